```python
import jax, jax.numpy as jnp
from jax import lax
import numpy as np

D_MODEL = 1024
BATCH = 4
SEQ = 8192
DEPTH = 4

N_MIXERS = 3
SB_HEADS = 16
SB_HEAD_DIM = D_MODEL // SB_HEADS
RET_HEADS = 4
RET_QK_DIM = D_MODEL // RET_HEADS
RET_V_DIM = 2 * RET_QK_DIM
RET_CHUNK = 128
ROPE_BASE = 10000.0
FOX_HEADS = 16
FOX_HEAD_DIM = D_MODEL // FOX_HEADS
FORGET_BIAS_INIT = 2.0
D_FF = 4 * D_MODEL
Q_BLOCK = 128
NORM_EPS = 1e-6
GN_EPS = 1e-5

kernel_name = 'hybrid_sb_retention_fox_trunk'


def rms_norm(x, g):
    xf = x.astype(jnp.float32)
    y = xf * lax.rsqrt(jnp.mean(xf * xf, axis=-1, keepdims=True) + NORM_EPS)
    return (y * g.astype(jnp.float32)).astype(x.dtype)


def split_heads(t, n_heads):
    b, s, _ = t.shape
    return t.reshape(b, s, n_heads, -1).transpose(0, 2, 1, 3)


def merge_heads(t):
    b, h, s, d = t.shape
    return t.transpose(0, 2, 1, 3).reshape(b, s, h * d)


def squared_relu_mlp(h, w_up, w_down):
    return jnp.square(jax.nn.relu(h @ w_up)) @ w_down


def stick_breaking_attention(q, k, v):
    s_len, dh = q.shape[2], q.shape[3]
    scale = dh ** -0.5
    outs = []
    for qs in range(0, s_len, Q_BLOCK):
        qe = qs + Q_BLOCK
        z = jnp.einsum('bhtd,bhsd->bhts', q[:, :, qs:qe], k[:, :, :qe]).astype(jnp.float32) * scale
        t_pos = jnp.arange(qs, qe)[:, None]
        s_pos = jnp.arange(qe)[None, :]
        strict = s_pos < t_pos
        log_keep = jnp.where(strict, jax.nn.log_sigmoid(-z), 0.0)
        later = lax.cumsum(log_keep, axis=3, reverse=True) - log_keep
        w = jnp.where(strict, jnp.exp(jax.nn.log_sigmoid(z) + later), 0.0)
        outs.append(jnp.einsum('bhts,bhsd->bhtd', w.astype(v.dtype), v[:, :, :qe]))
    return jnp.concatenate(outs, axis=2)


def stick_breaking_mixer(h, w_in, w_out):
    q, k, v = jnp.split(h @ w_in, 3, axis=-1)
    o = stick_breaking_attention(split_heads(q, SB_HEADS), split_heads(k, SB_HEADS),
                                 split_heads(v, SB_HEADS))
    return merge_heads(o) @ w_out


def rotary(x, positions):
    d = x.shape[-1]
    half = d // 2
    inv_freq = ROPE_BASE ** (-2.0 * jnp.arange(half, dtype=jnp.float32) / d)
    ang = positions[:, None].astype(jnp.float32) * inv_freq[None, :]
    cos, sin = jnp.cos(ang), jnp.sin(ang)
    x1, x2 = x[..., :half], x[..., half:]
    return jnp.concatenate([x1 * cos - x2 * sin, x1 * sin + x2 * cos], axis=-1)


def retention_chunkwise(q, k, v):
    b, n_h, s_len, dk = q.shape
    dv = v.shape[-1]
    c = RET_CHUNK
    n_chunks = s_len // c
    log_gamma = jnp.log1p(-jnp.exp2(-5.0 - jnp.arange(n_h, dtype=jnp.float32)))
    idx = jnp.arange(c, dtype=jnp.float32)
    diff = idx[:, None] - idx[None, :]
    intra = jnp.where(diff >= 0, jnp.exp(log_gamma[:, None, None] * jnp.maximum(diff, 0.0)), 0.0)
    q_decay = jnp.exp(log_gamma[:, None] * (idx + 1.0))
    k_decay = jnp.exp(log_gamma[:, None] * (c - 1.0 - idx))
    chunk_decay = jnp.exp(log_gamma * c)

    def to_chunks(t):
        return jnp.moveaxis(t.reshape(b, n_h, n_chunks, c, t.shape[-1]), 2, 0)

    def step(state, qkv):
        qc, kc, vc = qkv
        scores = jnp.einsum('bhid,bhjd->bhij', qc, kc) * intra
        out = (jnp.einsum('bhij,bhje->bhie', scores, vc)
               + jnp.einsum('bhid,bhde->bhie', qc * q_decay[:, :, None], state))
        state = (state * chunk_decay[:, None, None]
                 + jnp.einsum('bhjd,bhje->bhde', kc * k_decay[:, :, None], vc))
        return state, out

    state0 = jnp.zeros((b, n_h, dk, dv), jnp.float32)
    _, out = lax.scan(step, state0, (to_chunks(q), to_chunks(k), to_chunks(v)))
    return jnp.moveaxis(out, 0, 2).reshape(b, n_h, s_len, dv)


def retention_mixer(h, w_in, gn_gain, w_out):
    s_len = h.shape[1]
    qk_w = RET_HEADS * RET_QK_DIM
    v_w = RET_HEADS * RET_V_DIM
    q, k, v, g = jnp.split(h @ w_in, [qk_w, 2 * qk_w, 2 * qk_w + v_w], axis=-1)
    pos = jnp.arange(s_len)
    qh = rotary(split_heads(q, RET_HEADS).astype(jnp.float32), pos) * (RET_QK_DIM ** -0.5)
    kh = rotary(split_heads(k, RET_HEADS).astype(jnp.float32), pos)
    vh = split_heads(v, RET_HEADS).astype(jnp.float32)
    y = retention_chunkwise(qh, kh, vh)
    mu = jnp.mean(y, axis=-1, keepdims=True)
    var = jnp.mean(jnp.square(y - mu), axis=-1, keepdims=True)
    y = merge_heads((y - mu) * lax.rsqrt(var + GN_EPS)) * gn_gain.astype(jnp.float32)
    return (jax.nn.silu(g.astype(jnp.float32)) * y).astype(h.dtype) @ w_out


def forgetting_attention(q, k, v, cum_logf):
    s_len, dh = q.shape[2], q.shape[3]
    scale = dh ** -0.5
    outs = []
    for qs in range(0, s_len, Q_BLOCK):
        qe = qs + Q_BLOCK
        logits = (jnp.einsum('bhtd,bhsd->bhts', q[:, :, qs:qe], k[:, :, :qe]).astype(jnp.float32) * scale
                  + cum_logf[:, :, qs:qe, None] - cum_logf[:, :, None, :qe])
        causal = jnp.arange(qe)[None, :] <= jnp.arange(qs, qe)[:, None]
        p = jax.nn.softmax(jnp.where(causal, logits, -jnp.inf), axis=-1)
        outs.append(jnp.einsum('bhts,bhsd->bhtd', p.astype(v.dtype), v[:, :, :qe]))
    return jnp.concatenate(outs, axis=2)


def forgetting_mixer(h, w_in, b_forget, w_out):
    q, k, v, f = jnp.split(h @ w_in, [D_MODEL, 2 * D_MODEL, 3 * D_MODEL], axis=-1)
    log_f = jax.nn.log_sigmoid(f.astype(jnp.float32) + b_forget.astype(jnp.float32))
    cum_logf = jnp.cumsum(log_f, axis=1).transpose(0, 2, 1)
    o = forgetting_attention(split_heads(q, FOX_HEADS), split_heads(k, FOX_HEADS),
                             split_heads(v, FOX_HEADS), cum_logf)
    return merge_heads(o) @ w_out


MIXER_FNS = (stick_breaking_mixer, retention_mixer, forgetting_mixer)


def setup_inputs(seed: int = 0) -> dict:
    key = jax.random.key(seed)
    keys = iter(jax.random.split(key, 64))

    def normal(shape, scale):
        return jax.random.normal(next(keys), shape, jnp.float32) * scale

    def dense(fan_in, fan_out):
        return normal((fan_in, fan_out), fan_in ** -0.5)

    def gain(n):
        return 1.0 + normal((n,), 0.02)

    inputs = {'x': normal((BATCH, SEQ, D_MODEL), 1.0)}
    for i in range(DEPTH):
        kind = i % N_MIXERS
        inputs['norm_mix_%d' % i] = gain(D_MODEL)
        if kind == 0:
            inputs['w_in_%d' % i] = dense(D_MODEL, 3 * D_MODEL)
            inputs['w_out_%d' % i] = dense(D_MODEL, D_MODEL)
        elif kind == 1:
            width = 2 * RET_HEADS * RET_QK_DIM + 2 * RET_HEADS * RET_V_DIM
            inputs['w_in_%d' % i] = dense(D_MODEL, width)
            inputs['gn_gain_%d' % i] = gain(RET_HEADS * RET_V_DIM)
            inputs['w_out_%d' % i] = dense(RET_HEADS * RET_V_DIM, D_MODEL)
        else:
            inputs['w_in_%d' % i] = dense(D_MODEL, 3 * D_MODEL + FOX_HEADS)
            inputs['b_forget_%d' % i] = FORGET_BIAS_INIT + normal((FOX_HEADS,), 0.1)
            inputs['w_out_%d' % i] = dense(D_MODEL, D_MODEL)
        inputs['norm_mlp_%d' % i] = gain(D_MODEL)
        inputs['w_up_%d' % i] = dense(D_MODEL, D_FF)
        inputs['w_down_%d' % i] = dense(D_FF, D_MODEL)
    inputs['norm_final'] = gain(D_MODEL)
    return inputs


def reference(x,
              norm_mix_0, w_in_0, w_out_0, norm_mlp_0, w_up_0, w_down_0,
              norm_mix_1, w_in_1, gn_gain_1, w_out_1, norm_mlp_1, w_up_1, w_down_1,
              norm_mix_2, w_in_2, b_forget_2, w_out_2, norm_mlp_2, w_up_2, w_down_2,
              norm_mix_3, w_in_3, w_out_3, norm_mlp_3, w_up_3, w_down_3,
              norm_final):
    layers = [
        (norm_mix_0, (w_in_0, w_out_0), norm_mlp_0, w_up_0, w_down_0),
        (norm_mix_1, (w_in_1, gn_gain_1, w_out_1), norm_mlp_1, w_up_1, w_down_1),
        (norm_mix_2, (w_in_2, b_forget_2, w_out_2), norm_mlp_2, w_up_2, w_down_2),
        (norm_mix_3, (w_in_3, w_out_3), norm_mlp_3, w_up_3, w_down_3),
    ]
    for i in range(DEPTH):
        norm_mix, mix_params, norm_mlp, w_up, w_down = layers[i]
        x = x + MIXER_FNS[i % N_MIXERS](rms_norm(x, norm_mix), *mix_params)
        x = x + squared_relu_mlp(rms_norm(x, norm_mlp), w_up, w_down)
    return rms_norm(x, norm_final)
```

```python
import functools

import jax
import jax.numpy as jnp
import numpy as np
from jax import lax
from jax.experimental import pallas as pl
from jax.experimental.pallas import tpu as pltpu

F32 = jnp.float32
BF16 = jnp.bfloat16

D_MODEL = 1024
D_FF = 4 * D_MODEL
HEAD_DIM = 64
ATT_HEADS = D_MODEL // HEAD_DIM
LANES = 128
HEAD_PAIRS = D_MODEL // LANES
RET_HEADS = 4
RET_QK = D_MODEL // RET_HEADS
RET_V = 2 * RET_QK
RET_CHUNK = 128
ROPE_BASE = 10000.0
NORM_EPS = 1e-6
GN_EPS = 1e-5

ROW_TILE = 1024
ATT_TILE = 256
STEP_ROWS = 2 * ATT_TILE
RET_ROWS = 512
SKIP_BOUND = 105.0
VMEM_LIMIT = 56 * 1024 * 1024

_NT = (((1,), (1,)), ((), ()))
_TN = (((0,), (0,)), ((), ()))


def _params(*semantics):
    return pltpu.CompilerParams(dimension_semantics=semantics, vmem_limit_bytes=VMEM_LIMIT)


def _dot(a, b):
    return jnp.dot(a, b, preferred_element_type=F32)


def _rms_normed(x, gain):
    return x * lax.rsqrt(jnp.mean(x * x, axis=-1, keepdims=True) + NORM_EPS) * gain


def _softplus(x):
    return jnp.maximum(x, 0.0) + jnp.log(1.0 + jnp.exp(-jnp.abs(x)))


def _norm_matmul_kernel(x_ref, g_ref, w_ref, o_ref, h_ref):
    @pl.when(pl.program_id(1) == 0)
    def _():
        h_ref[...] = _rms_normed(x_ref[...], g_ref[...]).astype(BF16)

    o_ref[...] = _dot(h_ref[...], w_ref[...]).astype(o_ref.dtype)


def _norm_matmul(x, gain, w, out_dtype, col_tile):
    t, d = x.shape
    n = w.shape[1]
    return pl.pallas_call(
        _norm_matmul_kernel,
        grid=(t // ROW_TILE, n // col_tile),
        in_specs=[
            pl.BlockSpec((ROW_TILE, d), lambda i, j: (i, 0)),
            pl.BlockSpec((1, d), lambda i, j: (0, 0)),
            pl.BlockSpec((d, col_tile), lambda i, j: (0, j)),
        ],
        out_specs=pl.BlockSpec((ROW_TILE, col_tile), lambda i, j: (i, j)),
        out_shape=jax.ShapeDtypeStruct((t, n), out_dtype),
        scratch_shapes=[pltpu.VMEM((ROW_TILE, d), BF16)],
        compiler_params=_params("parallel", "arbitrary"),
    )(x, gain.reshape(1, d), w)


def _matmul_residual_kernel(a_ref, w_ref, r_ref, o_ref):
    o_ref[...] = r_ref[...] + _dot(a_ref[...], w_ref[...])


def _matmul_residual(a, w, res):
    t, k = a.shape
    d = w.shape[1]
    return pl.pallas_call(
        _matmul_residual_kernel,
        grid=(t // ROW_TILE,),
        in_specs=[
            pl.BlockSpec((ROW_TILE, k), lambda i: (i, 0)),
            pl.BlockSpec((k, d), lambda i: (0, 0)),
            pl.BlockSpec((ROW_TILE, d), lambda i: (i, 0)),
        ],
        out_specs=pl.BlockSpec((ROW_TILE, d), lambda i: (i, 0)),
        out_shape=jax.ShapeDtypeStruct((t, d), F32),
        compiler_params=_params("parallel"),
    )(a, w, res)


def _mlp_kernel(x_ref, g_ref, wu_ref, wd_ref, gf_ref, o_ref, h_ref, acc_ref, *, final_norm):
    f = pl.program_id(1)

    @pl.when(f == 0)
    def _():
        h_ref[...] = _rms_normed(x_ref[...], g_ref[...]).astype(BF16)
        acc_ref[...] = jnp.zeros_like(acc_ref)

    u = jnp.maximum(_dot(h_ref[...], wu_ref[...]), 0.0)
    acc_ref[...] += _dot((u * u).astype(BF16), wd_ref[...])

    @pl.when(f == pl.num_programs(1) - 1)
    def _():
        y = x_ref[...] + acc_ref[...]
        o_ref[...] = _rms_normed(y, gf_ref[...]) if final_norm else y


def _mlp(x, gain, w_up, w_down, final_gain, final_norm):
    t, d = x.shape
    ff = w_up.shape[1]
    ff_tile = 1024
    return pl.pallas_call(
        functools.partial(_mlp_kernel, final_norm=final_norm),
        grid=(t // ROW_TILE, ff // ff_tile),
        in_specs=[
            pl.BlockSpec((ROW_TILE, d), lambda i, f: (i, 0)),
            pl.BlockSpec((1, d), lambda i, f: (0, 0)),
            pl.BlockSpec((d, ff_tile), lambda i, f: (0, f)),
            pl.BlockSpec((ff_tile, d), lambda i, f: (f, 0)),
            pl.BlockSpec((1, d), lambda i, f: (0, 0)),
        ],
        out_specs=pl.BlockSpec((ROW_TILE, d), lambda i, f: (i, 0)),
        out_shape=jax.ShapeDtypeStruct((t, d), F32),
        scratch_shapes=[pltpu.VMEM((ROW_TILE, d), BF16), pltpu.VMEM((ROW_TILE, d), F32)],
        compiler_params=_params("parallel", "arbitrary"),
    )(x, gain.reshape(1, d), w_up, w_down, final_gain.reshape(1, d))


def _head_pair_masks():
    lane = lax.broadcasted_iota(jnp.int32, (1, LANES), 1)
    return lane < HEAD_DIM, lane >= HEAD_DIM


def _suffix_ones():
    idx = np.arange(ATT_TILE)
    return jnp.asarray(idx[:, None] >= idx[None, :], BF16)


def _stack_heads(x_heads, tiles):
    return jnp.concatenate([x_heads[h][t * ATT_TILE:(t + 1) * ATT_TILE]
                            for t in range(tiles) for h in range(2)], axis=0)


def _unstack_heads(stacked, first, tiles):
    parts = [jnp.where(first, stacked[(2 * t) * ATT_TILE:(2 * t + 1) * ATT_TILE],
                       stacked[(2 * t + 1) * ATT_TILE:(2 * t + 2) * ATT_TILE])
             for t in range(tiles)]
    return parts[0] if tiles == 1 else jnp.concatenate(parts, axis=0)


TILE_ROWS = 2 * ATT_TILE
STACK_ROWS = 2 * STEP_ROWS


def _stacked_diag_mask(n_rows, strict, later_tiles_open):
    row = lax.broadcasted_iota(jnp.int32, (n_rows, ATT_TILE), 0)
    col = lax.broadcasted_iota(jnp.int32, (n_rows, ATT_TILE), 1)
    pos = row & (ATT_TILE - 1)
    causal = (col < pos) if strict else (col <= pos)
    return jnp.logical_or(causal, row >= TILE_ROWS) if later_tiles_open else causal


def _sb_kernel(q_ref, k_ref, v_ref, tri_ref, o_ref, acc_ref, carry_ref):
    i = pl.program_id(2)
    first, second = _head_pair_masks()
    q = q_ref[0] * (HEAD_DIM ** -0.5)
    zero = jnp.zeros_like(q)
    q_stack = _stack_heads((jnp.where(first, q, zero), jnp.where(second, q, zero)), STEP_ROWS // ATT_TILE)
    tri = tri_ref[...]

    def visit(row0, n_rows, j, mask, gate):
        rows = pl.ds(row0, n_rows)
        start = pl.multiple_of(j * ATT_TILE, ATT_TILE)
        k = k_ref[0, pl.ds(start, ATT_TILE), :]
        v = v_ref[0, pl.ds(start, ATT_TILE), :]
        z = lax.dot_general(q_stack[row0:row0 + n_rows], k, _NT, preferred_element_type=F32)
        sp = _softplus(z)
        if mask is not None:
            sp = jnp.where(mask, sp, 0.0)
        hi = sp.astype(BF16)
        lo = (sp - hi.astype(F32)).astype(BF16)
        suffix = _dot(hi, tri) + _dot(lo, tri)
        w = jnp.exp(z - suffix - carry_ref[rows, :])
        if mask is not None:
            w = jnp.where(mask, w, 0.0)
        total = suffix[:, 0:1]
        if gate is not None:
            w, total = w * gate, total * gate
        acc_ref[rows, :] += _dot(w.astype(BF16), v)
        carry_ref[rows, :] += total

    acc_ref[...] = jnp.zeros_like(acc_ref)
    carry_ref[...] = jnp.zeros_like(carry_ref)
    first_block = (STEP_ROWS // ATT_TILE) * i
    visit(TILE_ROWS, TILE_ROWS, first_block + 1, _stacked_diag_mask(TILE_ROWS, True, False), None)
    visit(0, STACK_ROWS, first_block, _stacked_diag_mask(STACK_ROWS, True, True), None)
    visit(0, TILE_ROWS, jnp.maximum(first_block - 1, 0), None, (i > 0).astype(F32))

    def visit_earlier(tile):
        row0 = tile * TILE_ROWS

        def exhausted():
            return jnp.min(carry_ref[pl.ds(row0, TILE_ROWS), :]) > SKIP_BOUND

        def cond(state):
            j, done = state
            return jnp.logical_and(j >= 0, jnp.logical_not(done))

        def body(state):
            j, _ = state
            visit(row0, TILE_ROWS, j, None, None)
            return j - 1, exhausted()

        lax.while_loop(cond, body, (first_block + tile - 2, exhausted()))

    for tile in range(STEP_ROWS // ATT_TILE):
        visit_earlier(tile)
    o_ref[0] = _unstack_heads(acc_ref[...], first, STEP_ROWS // ATT_TILE).astype(o_ref.dtype)


def _sb_attention(qkv, batch, seq):
    return pl.pallas_call(
        _sb_kernel,
        grid=(batch, HEAD_PAIRS, seq // STEP_ROWS),
        in_specs=[
            pl.BlockSpec((1, STEP_ROWS, LANES), lambda b, p, i: (b, i, p)),
            pl.BlockSpec((1, seq, LANES), lambda b, p, i: (b, 0, HEAD_PAIRS + p)),
            pl.BlockSpec((1, seq, LANES), lambda b, p, i: (b, 0, 2 * HEAD_PAIRS + p)),
            pl.BlockSpec((ATT_TILE, ATT_TILE), lambda b, p, i: (0, 0)),
        ],
        out_specs=pl.BlockSpec((1, STEP_ROWS, LANES), lambda b, p, i: (b, i, p)),
        out_shape=jax.ShapeDtypeStruct((batch, seq, D_MODEL), BF16),
        scratch_shapes=[pltpu.VMEM((STACK_ROWS, LANES), F32), pltpu.VMEM((STACK_ROWS, 1), F32)],
        compiler_params=_params("parallel", "parallel", "arbitrary"),
    )(qkv, qkv, qkv, _suffix_ones())


def _split3(x):
    def top(v):
        return pltpu.bitcast(pltpu.bitcast(v, jnp.uint32) & jnp.uint32(0xFFFF0000), F32)
    a = top(x)
    r = x - a
    b = top(r)
    return a.astype(BF16), b.astype(BF16), (r - b).astype(BF16)


AUG_LANES = 6


def _fox_selectors():
    sel_q = np.zeros((HEAD_PAIRS, 3 * LANES, LANES), np.float32)
    sel_k = np.zeros((HEAD_PAIRS, 3 * LANES, LANES), np.float32)
    const_q = np.zeros((1, LANES), np.float32)
    const_k = np.zeros((1, LANES), np.float32)
    for m in range(2):
        for part in range(3):
            const_q[0, AUG_LANES * m + 3 + part] = 1.0
            const_k[0, AUG_LANES * m + part] = 1.0
            for p in range(HEAD_PAIRS):
                head = 2 * p + m
                sel_q[p, part * LANES + head, AUG_LANES * m + part] = 1.0
                sel_k[p, part * LANES + head, AUG_LANES * m + 3 + part] = -1.0
    return (jnp.asarray(sel_q, BF16), jnp.asarray(sel_k, BF16), jnp.asarray(const_q), jnp.asarray(const_k))


def _fox_cum_kernel(f_ref, b_ref, sq_ref, sk_ref, cq_ref, ck_ref, aq_ref, ak_ref, last_ref, carry_ref):
    @pl.when(pl.program_id(1) == 0)
    def _():
        carry_ref[...] = jnp.zeros_like(carry_ref)

    log_f = -_softplus(-(f_ref[0] + b_ref[...]))
    row = lax.broadcasted_iota(jnp.int32, (ATT_TILE, ATT_TILE), 0)
    col = lax.broadcasted_iota(jnp.int32, (ATT_TILE, ATT_TILE), 1)
    prefix_ones = (row >= col).astype(BF16)
    parts = _split3(log_f)
    cum = carry_ref[...] + _dot(prefix_ones, parts[0]) + _dot(prefix_ones, parts[1]) + _dot(prefix_ones, parts[2])
    carry_ref[...] = cum[ATT_TILE - 1:ATT_TILE, :]
    last_ref[0, 0] = cum[ATT_TILE - 1:ATT_TILE, :]
    stacked = jnp.concatenate(_split3(cum), axis=-1)
    for p in range(HEAD_PAIRS):
        aq_ref[0, p] = (_dot(stacked, sq_ref[p]) + cq_ref[...]).astype(BF16)
        ak_ref[0, p] = (_dot(stacked, sk_ref[p]) + ck_ref[...]).astype(BF16)


def _fox_cum(f_logits, bias, batch, seq):
    n_t = seq // ATT_TILE
    sel_q, sel_k, const_q, const_k = _fox_selectors()
    bias_row = jnp.zeros((1, LANES), F32).at[0, :ATT_HEADS].set(bias.astype(F32))
    whole = lambda shape: pl.BlockSpec(shape, lambda b, r: (0,) * len(shape))
    return pl.pallas_call(
        _fox_cum_kernel,
        grid=(batch, n_t),
        in_specs=[
            pl.BlockSpec((1, ATT_TILE, LANES), lambda b, r: (b, r, 0)),
            whole((1, LANES)),
            whole(sel_q.shape), whole(sel_k.shape), whole((1, LANES)), whole((1, LANES)),
        ],
        out_specs=[
            pl.BlockSpec((1, HEAD_PAIRS, ATT_TILE, LANES), lambda b, r: (b, 0, r, 0)),
            pl.BlockSpec((1, HEAD_PAIRS, ATT_TILE, LANES), lambda b, r: (b, 0, r, 0)),
            pl.BlockSpec((1, 1, 1, LANES), lambda b, r: (b, r, 0, 0)),
        ],
        out_shape=[
            jax.ShapeDtypeStruct((batch, HEAD_PAIRS, seq, LANES), BF16),
            jax.ShapeDtypeStruct((batch, HEAD_PAIRS, seq, LANES), BF16),
            jax.ShapeDtypeStruct((batch, n_t, 1, LANES), F32),
        ],
        scratch_shapes=[pltpu.VMEM((1, LANES), F32)],
        compiler_params=_params("parallel", "arbitrary"),
    )(f_logits, bias_row, sel_q, sel_k, const_q, const_k)


KNORM_ROWS = 512
FOX_TILES = 1


def _fox_kernel(last_ref, q_ref, aq_ref, k_ref, ak_ref, v_ref, o_ref, acc_ref, m_ref, l_ref, knorm_ref,
                *, n_blocks):
    b, p, i = pl.program_id(0), pl.program_id(1), pl.program_id(2)
    first, second = _head_pair_masks()
    head_lanes = (first, second)
    seq = k_ref.shape[1]

    @pl.when(i == 0)
    def _():
        def scan(c, best):
            kk = k_ref[0, pl.ds(pl.multiple_of(c * KNORM_ROWS, KNORM_ROWS), KNORM_ROWS), :].astype(F32)
            sq = kk * kk
            return tuple(jnp.maximum(best[h], jnp.sum(jnp.where(head_lanes[h], sq, 0.0), axis=-1, keepdims=True))
                         for h in range(2))
        init = (jnp.zeros((KNORM_ROWS, 1), F32),) * 2
        best = lax.fori_loop(0, seq // KNORM_ROWS, scan, init)
        for h in range(2):
            knorm_ref[h] = jnp.max(jnp.sqrt(best[h]))

    q = q_ref[0] * (HEAD_DIM ** -0.5)
    aq = aq_ref[0, 0]
    zero = jnp.zeros_like(q)
    lane = lax.broadcasted_iota(jnp.int32, (1, LANES), 1)
    aqf = aq.astype(F32)
    qf = q.astype(F32)
    q_heads, cum_t, q_norm = [], [], []
    for h in range(2):
        aug_lanes = jnp.logical_and(lane >= AUG_LANES * h, lane < AUG_LANES * (h + 1))
        q_heads.append(jnp.concatenate([jnp.where(head_lanes[h], q, zero), jnp.where(aug_lanes, aq, zero)], axis=-1))
        o = AUG_LANES * h
        cum_t.append(aqf[:, o:o + 1] + aqf[:, o + 1:o + 2] + aqf[:, o + 2:o + 3])
        q_norm.append(jnp.sqrt(jnp.sum(jnp.where(head_lanes[h], qf * qf, 0.0), axis=-1, keepdims=True)))
    tiles = FOX_TILES
    q_stack = _stack_heads(q_heads, tiles)
    first_block = tiles * i

    def visit(row0, n_rows, j, mask):
        start = pl.multiple_of(j * ATT_TILE, ATT_TILE)
        kk = jnp.concatenate([k_ref[0, pl.ds(start, ATT_TILE), :], ak_ref[0, 0, pl.ds(start, ATT_TILE), :]], axis=-1)
        v = v_ref[0, pl.ds(start, ATT_TILE), :]
        updates = []
        for c in range(n_rows // ATT_TILE):
            r0 = row0 + c * ATT_TILE
            rows = pl.ds(r0, ATT_TILE)
            s = lax.dot_general(q_stack[r0:r0 + ATT_TILE], kk, _NT, preferred_element_type=F32)
            if mask is not None:
                s = jnp.where(mask[c * ATT_TILE:(c + 1) * ATT_TILE], s, -1e30)
            m_old = m_ref[rows, :]
            m_new = jnp.maximum(m_old, jnp.max(s, axis=-1, keepdims=True))
            alpha = jnp.exp(m_old - m_new)
            prob = jnp.exp(s - jnp.concatenate([m_new, m_new], axis=-1))
            updates.append((rows, m_new, alpha * l_ref[rows, :] + jnp.sum(prob, axis=-1, keepdims=True),
                            alpha * acc_ref[rows, :] + _dot(prob.astype(BF16), v)))
        for rows, m_new, l_new, acc_new in updates:
            m_ref[rows, :], l_ref[rows, :], acc_ref[rows, :] = m_new, l_new, acc_new

    acc_ref[...] = jnp.zeros_like(acc_ref)
    l_ref[...] = jnp.zeros_like(l_ref)
    m_ref[...] = jnp.full_like(m_ref, -1e30)
    for t in range(tiles - 1, -1, -1):
        n_rows = (tiles - t) * TILE_ROWS
        visit(t * TILE_ROWS, n_rows, first_block + t, _stacked_diag_mask(n_rows, False, t < tiles - 1))

    m_now = m_ref[...]
    bound = []
    for h in range(2):
        slack = q_norm[h] * knorm_ref[h] + cum_t[h]
        per_tile = [jnp.max(slack[t * ATT_TILE:(t + 1) * ATT_TILE]
                            - m_now[(2 * t + h) * ATT_TILE:(2 * t + h + 1) * ATT_TILE]) for t in range(tiles)]
        bound.append(functools.reduce(jnp.maximum, per_tile))
    base = [(b * ATT_HEADS + 2 * p + h) * n_blocks for h in range(2)]

    def cond(j):
        jj = jnp.maximum(j, 0)
        live = jnp.logical_or(bound[0] - last_ref[base[0] + jj] >= -SKIP_BOUND,
                              bound[1] - last_ref[base[1] + jj] >= -SKIP_BOUND)
        return jnp.logical_and(j >= 0, live)

    def body(j):
        visit(0, tiles * TILE_ROWS, j, None)
        return j - 1

    lax.while_loop(cond, body, first_block - 1)
    o_ref[0] = _unstack_heads(acc_ref[...] / l_ref[...], first, tiles).astype(o_ref.dtype)


def _fox_attention(qkv, aug_q, aug_k, cum_last, batch, seq):
    step_rows = FOX_TILES * ATT_TILE
    grid_spec = pltpu.PrefetchScalarGridSpec(
        num_scalar_prefetch=1,
        grid=(batch, HEAD_PAIRS, seq // step_rows),
        in_specs=[
            pl.BlockSpec((1, step_rows, LANES), lambda b, p, i, t: (b, i, p)),
            pl.BlockSpec((1, 1, step_rows, LANES), lambda b, p, i, t: (b, p, i, 0)),
            pl.BlockSpec((1, seq, LANES), lambda b, p, i, t: (b, 0, HEAD_PAIRS + p)),
            pl.BlockSpec((1, 1, seq, LANES), lambda b, p, i, t: (b, p, 0, 0)),
            pl.BlockSpec((1, seq, LANES), lambda b, p, i, t: (b, 0, 2 * HEAD_PAIRS + p)),
        ],
        out_specs=pl.BlockSpec((1, step_rows, LANES), lambda b, p, i, t: (b, i, p)),
        scratch_shapes=[
            pltpu.VMEM((FOX_TILES * TILE_ROWS, LANES), F32),
            pltpu.VMEM((FOX_TILES * TILE_ROWS, LANES), F32),
            pltpu.VMEM((FOX_TILES * TILE_ROWS, LANES), F32),
            pltpu.SMEM((2,), F32),
        ],
    )
    return pl.pallas_call(
        functools.partial(_fox_kernel, n_blocks=seq // ATT_TILE),
        grid_spec=grid_spec,
        out_shape=jax.ShapeDtypeStruct((batch, seq, D_MODEL), BF16),
        compiler_params=_params("parallel", "parallel", "arbitrary"),
    )(cum_last, qkv, aug_q, qkv, aug_k, qkv)


def _retention_tables(seq):
    half = RET_QK // 2
    inv_freq = ROPE_BASE ** (-2.0 * jnp.arange(half, dtype=F32) / RET_QK)
    ang = jnp.arange(seq)[:, None].astype(F32) * inv_freq[None, :]
    log_gamma = jnp.log1p(-jnp.exp2(-5.0 - jnp.arange(RET_HEADS, dtype=F32)))
    idx = jnp.arange(RET_CHUNK, dtype=F32)
    diff = idx[:, None] - idx[None, :]
    intra = jnp.where(diff >= 0, jnp.exp(log_gamma[:, None, None] * jnp.maximum(diff, 0.0)), 0.0)
    q_decay = jnp.exp(log_gamma[:, None] * (idx + 1.0))[:, :, None]
    k_decay = jnp.exp(log_gamma[:, None] * (RET_CHUNK - 1.0 - idx))[:, :, None]
    chunk_decay = jnp.broadcast_to(jnp.exp(log_gamma * RET_CHUNK)[:, None, None], (RET_HEADS, 1, RET_V))
    return jnp.cos(ang), jnp.sin(ang), intra, q_decay, k_decay, chunk_decay


def _rotary(x, cos, sin):
    half = RET_QK // 2
    x1, x2 = x[:, :half], x[:, half:]
    return jnp.concatenate([x1 * cos - x2 * sin, x1 * sin + x2 * cos], axis=-1)


def _retention_kernel(q_ref, k_ref, v_ref, g_ref, cos_ref, sin_ref, intra_ref, qd_ref, kd_ref, cd_ref, gain_ref,
                      o_ref, state_ref):
    @pl.when(pl.program_id(2) == 0)
    def _():
        state_ref[...] = jnp.zeros_like(state_ref)

    intra, q_decay, k_decay, chunk_decay = intra_ref[0], qd_ref[0], kd_ref[0], cd_ref[0]
    for c in range(RET_ROWS // RET_CHUNK):
        rows = pl.ds(c * RET_CHUNK, RET_CHUNK)
        cos, sin = cos_ref[rows, :], sin_ref[rows, :]
        qc = _rotary(q_ref[0, rows, :].astype(F32), cos, sin) * (RET_QK ** -0.5)
        kc = _rotary(k_ref[0, rows, :].astype(F32), cos, sin)
        vc = v_ref[0, rows, :]
        state = state_ref[...]
        scores = lax.dot_general(qc.astype(BF16), kc.astype(BF16), _NT, preferred_element_type=F32) * intra
        y = _dot(scores.astype(BF16), vc) + _dot((qc * q_decay).astype(BF16), state.astype(BF16))
        state_ref[...] = state * chunk_decay + lax.dot_general(
            (kc * k_decay).astype(BF16), vc, _TN, preferred_element_type=F32)
        mu = jnp.mean(y, axis=-1, keepdims=True)
        yc = y - mu
        var = jnp.mean(yc * yc, axis=-1, keepdims=True)
        yn = yc * lax.rsqrt(var + GN_EPS) * gain_ref[...]
        g = g_ref[0, rows, :].astype(F32)
        o_ref[0, rows, :] = (g / (1.0 + jnp.exp(-g)) * yn).astype(o_ref.dtype)


def _retention(proj, gn_gain, batch, seq):
    cos, sin, intra, q_decay, k_decay, chunk_decay = _retention_tables(seq)
    qk_blocks = D_MODEL // RET_QK
    v_blocks = 2 * D_MODEL // RET_V
    per_head = lambda shape: pl.BlockSpec((1,) + shape, lambda b, h, r: (h, 0, 0))
    return pl.pallas_call(
        _retention_kernel,
        grid=(batch, RET_HEADS, seq // RET_ROWS),
        in_specs=[
            pl.BlockSpec((1, RET_ROWS, RET_QK), lambda b, h, r: (b, r, h)),
            pl.BlockSpec((1, RET_ROWS, RET_QK), lambda b, h, r: (b, r, qk_blocks + h)),
            pl.BlockSpec((1, RET_ROWS, RET_V), lambda b, h, r: (b, r, v_blocks + h)),
            pl.BlockSpec((1, RET_ROWS, RET_V), lambda b, h, r: (b, r, v_blocks + RET_HEADS + h)),
            pl.BlockSpec((RET_ROWS, RET_QK // 2), lambda b, h, r: (r, 0)),
            pl.BlockSpec((RET_ROWS, RET_QK // 2), lambda b, h, r: (r, 0)),
            per_head((RET_CHUNK, RET_CHUNK)),
            per_head((RET_CHUNK, 1)),
            per_head((RET_CHUNK, 1)),
            per_head((1, RET_V)),
            pl.BlockSpec((1, RET_V), lambda b, h, r: (0, h)),
        ],
        out_specs=pl.BlockSpec((1, RET_ROWS, RET_V), lambda b, h, r: (b, r, h)),
        out_shape=jax.ShapeDtypeStruct((batch, seq, RET_HEADS * RET_V), BF16),
        scratch_shapes=[pltpu.VMEM((RET_QK, RET_V), F32)],
        compiler_params=_params("parallel", "parallel", "arbitrary"),
    )(proj, proj, proj, proj, cos, sin, intra, q_decay, k_decay, chunk_decay,
      gn_gain.astype(F32).reshape(1, RET_HEADS * RET_V))


def _sb_layer(x, norm, w_in, w_out, batch, seq):
    qkv = _norm_matmul(x, norm, w_in.astype(BF16), BF16, 1024)
    o = _sb_attention(qkv.reshape(batch, seq, -1), batch, seq)
    return _matmul_residual(o.reshape(batch * seq, -1), w_out.astype(BF16), x)


def _retention_layer(x, norm, w_in, gn_gain, w_out, batch, seq):
    proj = _norm_matmul(x, norm, w_in.astype(BF16), BF16, 1024)
    y = _retention(proj.reshape(batch, seq, -1), gn_gain, batch, seq)
    return _matmul_residual(y.reshape(batch * seq, -1), w_out.astype(BF16), x)


def _fox_layer(x, norm, w_in, b_forget, w_out, batch, seq):
    w_qkv = w_in[:, :3 * D_MODEL].astype(BF16)
    w_f = jnp.pad(w_in[:, 3 * D_MODEL:], ((0, 0), (0, LANES - ATT_HEADS))).astype(BF16)
    qkv = _norm_matmul(x, norm, w_qkv, BF16, 1024)
    f_logits = _norm_matmul(x, norm, w_f, F32, LANES)
    aug_q, aug_k, cum_last = _fox_cum(f_logits.reshape(batch, seq, LANES), b_forget, batch, seq)
    last_table = cum_last[:, :, 0, :ATT_HEADS].transpose(0, 2, 1).reshape(-1)
    o = _fox_attention(qkv.reshape(batch, seq, -1), aug_q, aug_k, last_table, batch, seq)
    return _matmul_residual(o.reshape(batch * seq, -1), w_out.astype(BF16), x)


def kernel(x, norm_mix_0, w_in_0, w_out_0, norm_mlp_0, w_up_0, w_down_0, norm_mix_1, w_in_1, gn_gain_1, w_out_1, norm_mlp_1, w_up_1, w_down_1, norm_mix_2, w_in_2, b_forget_2, w_out_2, norm_mlp_2, w_up_2, w_down_2, norm_mix_3, w_in_3, w_out_3, norm_mlp_3, w_up_3, w_down_3, norm_final):
    batch, seq, d = x.shape
    h = x.reshape(batch * seq, d)
    mlps = ((norm_mlp_0, w_up_0, w_down_0), (norm_mlp_1, w_up_1, w_down_1),
            (norm_mlp_2, w_up_2, w_down_2), (norm_mlp_3, w_up_3, w_down_3))
    for layer in range(4):
        if layer == 0:
            h = _sb_layer(h, norm_mix_0, w_in_0, w_out_0, batch, seq)
        elif layer == 1:
            h = _retention_layer(h, norm_mix_1, w_in_1, gn_gain_1, w_out_1, batch, seq)
        elif layer == 2:
            h = _fox_layer(h, norm_mix_2, w_in_2, b_forget_2, w_out_2, batch, seq)
        else:
            h = _sb_layer(h, norm_mix_3, w_in_3, w_out_3, batch, seq)
        norm_mlp, w_up, w_down = mlps[layer]
        h = _mlp(h, norm_mlp, w_up.astype(BF16), w_down.astype(BF16), norm_final, layer == 3)
    return h.reshape(batch, seq, d)
```

```python
import functools

import jax
import jax.numpy as jnp
import numpy as np
from jax import lax
from jax.experimental import pallas as pl
from jax.experimental.pallas import tpu as pltpu

F32 = jnp.float32
BF16 = jnp.bfloat16

D_MODEL = 1024
D_FF = 4 * D_MODEL
HEAD_DIM = 64
ATT_HEADS = D_MODEL // HEAD_DIM
LANES = 128
HEAD_PAIRS = D_MODEL // LANES
RET_HEADS = 4
RET_QK = D_MODEL // RET_HEADS
RET_V = 2 * RET_QK
RET_CHUNK = 128
ROPE_BASE = 10000.0
NORM_EPS = 1e-6
GN_EPS = 1e-5

ROW_TILE = 1024
ATT_TILE = 256
STEP_ROWS = 2 * ATT_TILE
RET_ROWS = 512
SKIP_BOUND = 105.0
VMEM_LIMIT = 56 * 1024 * 1024

_NT = (((1,), (1,)), ((), ()))
_TN = (((0,), (0,)), ((), ()))


def _params(*semantics):
    return pltpu.CompilerParams(dimension_semantics=semantics, vmem_limit_bytes=VMEM_LIMIT)


def _dot(a, b):
    return jnp.dot(a, b, preferred_element_type=F32)


def _rms_normed(x, gain):
    return x * lax.rsqrt(jnp.mean(x * x, axis=-1, keepdims=True) + NORM_EPS) * gain


def _softplus(x):
    return jnp.maximum(x, 0.0) + jnp.log(1.0 + jnp.exp(-jnp.abs(x)))


def _norm_matmul_kernel(x_ref, g_ref, w_ref, o_ref, h_ref):
    @pl.when(pl.program_id(1) == 0)
    def _():
        h_ref[...] = _rms_normed(x_ref[...], g_ref[...]).astype(BF16)

    o_ref[...] = _dot(h_ref[...], w_ref[...]).astype(o_ref.dtype)


def _norm_matmul(x, gain, w, out_dtype, col_tile):
    t, d = x.shape
    n = w.shape[1]
    return pl.pallas_call(
        _norm_matmul_kernel,
        grid=(t // ROW_TILE, n // col_tile),
        in_specs=[
            pl.BlockSpec((ROW_TILE, d), lambda i, j: (i, 0)),
            pl.BlockSpec((1, d), lambda i, j: (0, 0)),
            pl.BlockSpec((d, col_tile), lambda i, j: (0, j)),
        ],
        out_specs=pl.BlockSpec((ROW_TILE, col_tile), lambda i, j: (i, j)),
        out_shape=jax.ShapeDtypeStruct((t, n), out_dtype),
        scratch_shapes=[pltpu.VMEM((ROW_TILE, d), BF16)],
        compiler_params=_params("parallel", "arbitrary"),
    )(x, gain.reshape(1, d), w)


def _norm_matmul_gate_kernel(x_ref, g_ref, w_ref, wf_ref, o_ref, f_ref, h_ref):
    @pl.when(pl.program_id(1) == 0)
    def _():
        h_ref[...] = _rms_normed(x_ref[...], g_ref[...]).astype(BF16)
        f_ref[...] = _dot(h_ref[...], wf_ref[...])

    o_ref[...] = _dot(h_ref[...], w_ref[...]).astype(o_ref.dtype)


def _norm_matmul_gate(x, gain, w, w_gate, col_tile):
    t, d = x.shape
    n = w.shape[1]
    return pl.pallas_call(
        _norm_matmul_gate_kernel,
        grid=(t // ROW_TILE, n // col_tile),
        in_specs=[
            pl.BlockSpec((ROW_TILE, d), lambda i, j: (i, 0)),
            pl.BlockSpec((1, d), lambda i, j: (0, 0)),
            pl.BlockSpec((d, col_tile), lambda i, j: (0, j)),
            pl.BlockSpec((d, LANES), lambda i, j: (0, 0)),
        ],
        out_specs=[pl.BlockSpec((ROW_TILE, col_tile), lambda i, j: (i, j)),
                   pl.BlockSpec((ROW_TILE, LANES), lambda i, j: (i, 0))],
        out_shape=[jax.ShapeDtypeStruct((t, n), BF16), jax.ShapeDtypeStruct((t, LANES), F32)],
        scratch_shapes=[pltpu.VMEM((ROW_TILE, d), BF16)],
        compiler_params=_params("parallel", "arbitrary"),
    )(x, gain.reshape(1, d), w, w_gate)


def _mixer_out_mlp_kernel(a_ref, wo_ref, x_ref, g_ref, wu_ref, wd_ref, gf_ref, o_ref, h_ref, *, final_norm):
    f = pl.program_id(1)

    @pl.when(f == 0)
    def _():
        o_ref[...] = x_ref[...] + _dot(a_ref[...], wo_ref[...])
        h_ref[...] = _rms_normed(o_ref[...], g_ref[...]).astype(BF16)

    u = jnp.maximum(_dot(h_ref[...], wu_ref[...]), 0.0)
    o_ref[...] += _dot((u * u).astype(BF16), wd_ref[...])

    if final_norm:
        @pl.when(f == pl.num_programs(1) - 1)
        def _():
            o_ref[...] = _rms_normed(o_ref[...], gf_ref[...])


def _mixer_out_mlp(a, w_out, x, gain, w_up, w_down, final_gain, final_norm):
    t, d = x.shape
    k = a.shape[1]
    ff = w_up.shape[1]
    ff_tile = 1024
    return pl.pallas_call(
        functools.partial(_mixer_out_mlp_kernel, final_norm=final_norm),
        grid=(t // ROW_TILE, ff // ff_tile),
        in_specs=[
            pl.BlockSpec((ROW_TILE, k), lambda i, f: (i, 0)),
            pl.BlockSpec((k, d), lambda i, f: (0, 0)),
            pl.BlockSpec((ROW_TILE, d), lambda i, f: (i, 0)),
            pl.BlockSpec((1, d), lambda i, f: (0, 0)),
            pl.BlockSpec((d, ff_tile), lambda i, f: (0, f)),
            pl.BlockSpec((ff_tile, d), lambda i, f: (f, 0)),
            pl.BlockSpec((1, d), lambda i, f: (0, 0)),
        ],
        out_specs=pl.BlockSpec((ROW_TILE, d), lambda i, f: (i, 0)),
        out_shape=jax.ShapeDtypeStruct((t, d), F32),
        scratch_shapes=[pltpu.VMEM((ROW_TILE, d), BF16)],
        compiler_params=_params("parallel", "arbitrary"),
    )(a, w_out, x, gain.reshape(1, d), w_up, w_down, final_gain.reshape(1, d))


def _head_pair_masks():
    lane = lax.broadcasted_iota(jnp.int32, (1, LANES), 1)
    return lane < HEAD_DIM, lane >= HEAD_DIM


def _suffix_ones():
    idx = np.arange(ATT_TILE)
    return jnp.asarray(idx[:, None] >= idx[None, :], BF16)


def _stack_heads(x_heads, tiles):
    return jnp.concatenate([x_heads[h][t * ATT_TILE:(t + 1) * ATT_TILE]
                            for t in range(tiles) for h in range(2)], axis=0)


def _unstack_heads(stacked, first, tiles):
    parts = [jnp.where(first, stacked[(2 * t) * ATT_TILE:(2 * t + 1) * ATT_TILE],
                       stacked[(2 * t + 1) * ATT_TILE:(2 * t + 2) * ATT_TILE])
             for t in range(tiles)]
    return parts[0] if tiles == 1 else jnp.concatenate(parts, axis=0)


TILE_ROWS = 2 * ATT_TILE
STACK_ROWS = 2 * STEP_ROWS


def _stacked_diag_mask(n_rows, strict, later_tiles_open):
    row = lax.broadcasted_iota(jnp.int32, (n_rows, ATT_TILE), 0)
    col = lax.broadcasted_iota(jnp.int32, (n_rows, ATT_TILE), 1)
    pos = row & (ATT_TILE - 1)
    causal = (col < pos) if strict else (col <= pos)
    return jnp.logical_or(causal, row >= TILE_ROWS) if later_tiles_open else causal


def _sb_kernel(q_ref, k_ref, v_ref, tri_ref, o_ref, acc_ref, carry_ref):
    i = pl.program_id(2)
    first, second = _head_pair_masks()
    q = q_ref[0] * (HEAD_DIM ** -0.5)
    zero = jnp.zeros_like(q)
    q_stack = _stack_heads((jnp.where(first, q, zero), jnp.where(second, q, zero)), STEP_ROWS // ATT_TILE)
    tri = tri_ref[...]

    def visit(row0, n_rows, j, mask, gate):
        rows = pl.ds(row0, n_rows)
        start = pl.multiple_of(j * ATT_TILE, ATT_TILE)
        k = k_ref[0, pl.ds(start, ATT_TILE), :]
        v = v_ref[0, pl.ds(start, ATT_TILE), :]
        z = lax.dot_general(q_stack[row0:row0 + n_rows], k, _NT, preferred_element_type=F32)
        sp = _softplus(z)
        if mask is not None:
            sp = jnp.where(mask, sp, 0.0)
        hi = sp.astype(BF16)
        lo = (sp - hi.astype(F32)).astype(BF16)
        suffix = _dot(hi, tri) + _dot(lo, tri)
        w = jnp.exp(z - suffix - carry_ref[rows, :])
        if mask is not None:
            w = jnp.where(mask, w, 0.0)
        total = suffix[:, 0:1]
        if gate is not None:
            w, total = w * gate, total * gate
        acc_ref[rows, :] += _dot(w.astype(BF16), v)
        carry_ref[rows, :] += total

    acc_ref[...] = jnp.zeros_like(acc_ref)
    carry_ref[...] = jnp.zeros_like(carry_ref)
    first_block = (STEP_ROWS // ATT_TILE) * i
    visit(TILE_ROWS, TILE_ROWS, first_block + 1, _stacked_diag_mask(TILE_ROWS, True, False), None)
    visit(0, STACK_ROWS, first_block, _stacked_diag_mask(STACK_ROWS, True, True), None)
    visit(0, TILE_ROWS, jnp.maximum(first_block - 1, 0), None, (i > 0).astype(F32))

    def visit_earlier(tile):
        row0 = tile * TILE_ROWS

        def exhausted():
            return jnp.min(carry_ref[pl.ds(row0, TILE_ROWS), :]) > SKIP_BOUND

        def cond(state):
            j, done = state
            return jnp.logical_and(j >= 0, jnp.logical_not(done))

        def body(state):
            j, _ = state
            visit(row0, TILE_ROWS, j, None, None)
            return j - 1, exhausted()

        lax.while_loop(cond, body, (first_block + tile - 2, exhausted()))

    for tile in range(STEP_ROWS // ATT_TILE):
        visit_earlier(tile)
    o_ref[0] = _unstack_heads(acc_ref[...], first, STEP_ROWS // ATT_TILE).astype(o_ref.dtype)


def _sb_attention(qkv, batch, seq):
    return pl.pallas_call(
        _sb_kernel,
        grid=(batch, HEAD_PAIRS, seq // STEP_ROWS),
        in_specs=[
            pl.BlockSpec((1, STEP_ROWS, LANES), lambda b, p, i: (b, i, p)),
            pl.BlockSpec((1, seq, LANES), lambda b, p, i: (b, 0, HEAD_PAIRS + p)),
            pl.BlockSpec((1, seq, LANES), lambda b, p, i: (b, 0, 2 * HEAD_PAIRS + p)),
            pl.BlockSpec((ATT_TILE, ATT_TILE), lambda b, p, i: (0, 0)),
        ],
        out_specs=pl.BlockSpec((1, STEP_ROWS, LANES), lambda b, p, i: (b, i, p)),
        out_shape=jax.ShapeDtypeStruct((batch, seq, D_MODEL), BF16),
        scratch_shapes=[pltpu.VMEM((STACK_ROWS, LANES), F32), pltpu.VMEM((STACK_ROWS, 1), F32)],
        compiler_params=_params("parallel", "parallel", "arbitrary"),
    )(qkv, qkv, qkv, _suffix_ones())


def _split3(x):
    def top(v):
        return pltpu.bitcast(pltpu.bitcast(v, jnp.uint32) & jnp.uint32(0xFFFF0000), F32)
    a = top(x)
    r = x - a
    b = top(r)
    return a.astype(BF16), b.astype(BF16), (r - b).astype(BF16)


AUG_LANES = 6


def _fox_selectors():
    sel_q = np.zeros((HEAD_PAIRS, 3 * LANES, LANES), np.float32)
    sel_k = np.zeros((HEAD_PAIRS, 3 * LANES, LANES), np.float32)
    const_q = np.zeros((1, LANES), np.float32)
    const_k = np.zeros((1, LANES), np.float32)
    for m in range(2):
        for part in range(3):
            const_q[0, AUG_LANES * m + 3 + part] = 1.0
            const_k[0, AUG_LANES * m + part] = 1.0
            for p in range(HEAD_PAIRS):
                head = 2 * p + m
                sel_q[p, part * LANES + head, AUG_LANES * m + part] = 1.0
                sel_k[p, part * LANES + head, AUG_LANES * m + 3 + part] = -1.0
    return (jnp.asarray(sel_q, BF16), jnp.asarray(sel_k, BF16), jnp.asarray(const_q), jnp.asarray(const_k))


def _fox_cum_kernel(f_ref, b_ref, sq_ref, sk_ref, cq_ref, ck_ref, aq_ref, ak_ref, last_ref, carry_ref):
    @pl.when(pl.program_id(1) == 0)
    def _():
        carry_ref[...] = jnp.zeros_like(carry_ref)

    log_f = -_softplus(-(f_ref[0] + b_ref[...]))
    row = lax.broadcasted_iota(jnp.int32, (ATT_TILE, ATT_TILE), 0)
    col = lax.broadcasted_iota(jnp.int32, (ATT_TILE, ATT_TILE), 1)
    prefix_ones = (row >= col).astype(BF16)
    parts = _split3(log_f)
    cum = carry_ref[...] + _dot(prefix_ones, parts[0]) + _dot(prefix_ones, parts[1]) + _dot(prefix_ones, parts[2])
    carry_ref[...] = cum[ATT_TILE - 1:ATT_TILE, :]
    last_ref[0, 0] = cum[ATT_TILE - 1:ATT_TILE, :]
    stacked = jnp.concatenate(_split3(cum), axis=-1)
    for p in range(HEAD_PAIRS):
        aq_ref[0, p] = (_dot(stacked, sq_ref[p]) + cq_ref[...]).astype(BF16)
        ak_ref[0, p] = (_dot(stacked, sk_ref[p]) + ck_ref[...]).astype(BF16)


def _fox_cum(f_logits, bias, batch, seq):
    n_t = seq // ATT_TILE
    sel_q, sel_k, const_q, const_k = _fox_selectors()
    bias_row = jnp.zeros((1, LANES), F32).at[0, :ATT_HEADS].set(bias.astype(F32))
    whole = lambda shape: pl.BlockSpec(shape, lambda b, r: (0,) * len(shape))
    return pl.pallas_call(
        _fox_cum_kernel,
        grid=(batch, n_t),
        in_specs=[
            pl.BlockSpec((1, ATT_TILE, LANES), lambda b, r: (b, r, 0)),
            whole((1, LANES)),
            whole(sel_q.shape), whole(sel_k.shape), whole((1, LANES)), whole((1, LANES)),
        ],
        out_specs=[
            pl.BlockSpec((1, HEAD_PAIRS, ATT_TILE, LANES), lambda b, r: (b, 0, r, 0)),
            pl.BlockSpec((1, HEAD_PAIRS, ATT_TILE, LANES), lambda b, r: (b, 0, r, 0)),
            pl.BlockSpec((1, 1, 1, LANES), lambda b, r: (b, r, 0, 0)),
        ],
        out_shape=[
            jax.ShapeDtypeStruct((batch, HEAD_PAIRS, seq, LANES), BF16),
            jax.ShapeDtypeStruct((batch, HEAD_PAIRS, seq, LANES), BF16),
            jax.ShapeDtypeStruct((batch, n_t, 1, LANES), F32),
        ],
        scratch_shapes=[pltpu.VMEM((1, LANES), F32)],
        compiler_params=_params("parallel", "arbitrary"),
    )(f_logits, bias_row, sel_q, sel_k, const_q, const_k)


KNORM_ROWS = 512


def _fox_kernel(last_ref, q_ref, aq_ref, k_ref, ak_ref, v_ref, o_ref, acc_ref, m_ref, l_ref, knorm_ref,
                *, n_blocks):
    b, p, i = pl.program_id(0), pl.program_id(1), pl.program_id(2)
    first, second = _head_pair_masks()
    head_lanes = (first, second)
    seq = k_ref.shape[1]

    @pl.when(i == 0)
    def _():
        def scan(c, best):
            kk = k_ref[0, pl.ds(pl.multiple_of(c * KNORM_ROWS, KNORM_ROWS), KNORM_ROWS), :].astype(F32)
            sq = kk * kk
            return tuple(jnp.maximum(best[h], jnp.sum(jnp.where(head_lanes[h], sq, 0.0), axis=-1, keepdims=True))
                         for h in range(2))
        init = (jnp.zeros((KNORM_ROWS, 1), F32),) * 2
        best = lax.fori_loop(0, seq // KNORM_ROWS, scan, init)
        for h in range(2):
            knorm_ref[h] = jnp.max(jnp.sqrt(best[h]))

    q = q_ref[0] * (HEAD_DIM ** -0.5)
    aq = aq_ref[0, 0]
    zero = jnp.zeros_like(q)
    lane = lax.broadcasted_iota(jnp.int32, (1, LANES), 1)
    aqf = aq.astype(F32)
    qf = q.astype(F32)
    q_heads, cum_t, q_norm = [], [], []
    for h in range(2):
        aug_lanes = jnp.logical_and(lane >= AUG_LANES * h, lane < AUG_LANES * (h + 1))
        q_heads.append(jnp.concatenate([jnp.where(head_lanes[h], q, zero), jnp.where(aug_lanes, aq, zero)], axis=-1))
        o = AUG_LANES * h
        cum_t.append(aqf[:, o:o + 1] + aqf[:, o + 1:o + 2] + aqf[:, o + 2:o + 3])
        q_norm.append(jnp.sqrt(jnp.sum(jnp.where(head_lanes[h], qf * qf, 0.0), axis=-1, keepdims=True)))
    row = lax.broadcasted_iota(jnp.int32, (ATT_TILE, ATT_TILE), 0)
    col = lax.broadcasted_iota(jnp.int32, (ATT_TILE, ATT_TILE), 1)
    causal = col <= row

    def local_softmax(h, j, mask, present):
        start = pl.multiple_of(jnp.maximum(j, 0) * ATT_TILE, ATT_TILE)
        kk = jnp.concatenate([k_ref[0, pl.ds(start, ATT_TILE), :], ak_ref[0, 0, pl.ds(start, ATT_TILE), :]], axis=-1)
        v = v_ref[0, pl.ds(start, ATT_TILE), :]
        s = lax.dot_general(q_heads[h], kk, _NT, preferred_element_type=F32)
        if mask is not None:
            s = jnp.where(mask, s, -1e30)
        if present is not None:
            s = jnp.where(present, s, -1e30)
        m = jnp.broadcast_to(jnp.max(s, axis=-1, keepdims=True), (ATT_TILE, LANES))
        prob = jnp.exp(s - jnp.concatenate([m, m], axis=-1))
        total = jnp.broadcast_to(jnp.sum(prob, axis=-1, keepdims=True), (ATT_TILE, LANES))
        return m, total, _dot(prob.astype(BF16), v)

    def merge(h, state, parts):
        m_old, l_old, acc_old = state
        m_new = functools.reduce(jnp.maximum, [part[0] for part in parts], m_old)
        scale = jnp.exp(m_old - m_new)
        l_new, acc_new = scale * l_old, scale * acc_old
        for m, total, pv in parts:
            scale = jnp.exp(m - m_new)
            l_new, acc_new = l_new + scale * total, acc_new + scale * pv
        m_ref[h], l_ref[h], acc_ref[h] = m_new, l_new, acc_new

    empty = (jnp.full((ATT_TILE, LANES), -1e30, F32), jnp.zeros((ATT_TILE, LANES), F32),
             jnp.zeros((ATT_TILE, LANES), F32))
    for h in range(2):
        merge(h, empty, [local_softmax(h, i, causal, None), local_softmax(h, i - 1, None, i > 0)])

    bound = [jnp.max(q_norm[h] * knorm_ref[h] + cum_t[h] - m_ref[h]) for h in range(2)]
    base = [(b * ATT_HEADS + 2 * p + h) * n_blocks for h in range(2)]

    def cond(j):
        jj = jnp.maximum(j, 0)
        live = jnp.logical_or(bound[0] - last_ref[base[0] + jj] >= -SKIP_BOUND,
                              bound[1] - last_ref[base[1] + jj] >= -SKIP_BOUND)
        return jnp.logical_and(j >= 0, live)

    def body(j):
        parts = [[local_softmax(h, j, None, None), local_softmax(h, j - 1, None, j > 0)] for h in range(2)]
        states = [(m_ref[h], l_ref[h], acc_ref[h]) for h in range(2)]
        for h in range(2):
            merge(h, states[h], parts[h])
        return j - 2

    lax.while_loop(cond, body, i - 2)
    o_ref[0] = jnp.where(first, acc_ref[0] / l_ref[0], acc_ref[1] / l_ref[1]).astype(o_ref.dtype)


def _fox_attention(qkv, aug_q, aug_k, cum_last, batch, seq):
    grid_spec = pltpu.PrefetchScalarGridSpec(
        num_scalar_prefetch=1,
        grid=(batch, HEAD_PAIRS, seq // ATT_TILE),
        in_specs=[
            pl.BlockSpec((1, ATT_TILE, LANES), lambda b, p, i, t: (b, i, p)),
            pl.BlockSpec((1, 1, ATT_TILE, LANES), lambda b, p, i, t: (b, p, i, 0)),
            pl.BlockSpec((1, seq, LANES), lambda b, p, i, t: (b, 0, HEAD_PAIRS + p)),
            pl.BlockSpec((1, 1, seq, LANES), lambda b, p, i, t: (b, p, 0, 0)),
            pl.BlockSpec((1, seq, LANES), lambda b, p, i, t: (b, 0, 2 * HEAD_PAIRS + p)),
        ],
        out_specs=pl.BlockSpec((1, ATT_TILE, LANES), lambda b, p, i, t: (b, i, p)),
        scratch_shapes=[
            pltpu.VMEM((2, ATT_TILE, LANES), F32),
            pltpu.VMEM((2, ATT_TILE, LANES), F32),
            pltpu.VMEM((2, ATT_TILE, LANES), F32),
            pltpu.SMEM((2,), F32),
        ],
    )
    return pl.pallas_call(
        functools.partial(_fox_kernel, n_blocks=seq // ATT_TILE),
        grid_spec=grid_spec,
        out_shape=jax.ShapeDtypeStruct((batch, seq, D_MODEL), BF16),
        compiler_params=_params("parallel", "parallel", "arbitrary"),
    )(cum_last, qkv, aug_q, qkv, aug_k, qkv)


def _retention_tables(seq):
    half = RET_QK // 2
    inv_freq = ROPE_BASE ** (-2.0 * jnp.arange(half, dtype=F32) / RET_QK)
    ang = jnp.arange(seq)[:, None].astype(F32) * inv_freq[None, :]
    log_gamma = jnp.log1p(-jnp.exp2(-5.0 - jnp.arange(RET_HEADS, dtype=F32)))
    idx = jnp.arange(RET_CHUNK, dtype=F32)
    diff = idx[:, None] - idx[None, :]
    intra = jnp.where(diff >= 0, jnp.exp(log_gamma[:, None, None] * jnp.maximum(diff, 0.0)), 0.0)
    q_decay = jnp.exp(log_gamma[:, None] * (idx + 1.0))[:, :, None]
    k_decay = jnp.exp(log_gamma[:, None] * (RET_CHUNK - 1.0 - idx))[:, :, None]
    chunk_decay = jnp.broadcast_to(jnp.exp(log_gamma * RET_CHUNK)[:, None, None], (RET_HEADS, 1, RET_V))
    return jnp.cos(ang), jnp.sin(ang), intra, q_decay, k_decay, chunk_decay


def _rotary(x, cos, sin):
    half = RET_QK // 2
    x1, x2 = x[:, :half], x[:, half:]
    return jnp.concatenate([x1 * cos - x2 * sin, x1 * sin + x2 * cos], axis=-1)


def _retention_kernel(q_ref, k_ref, v_ref, g_ref, cos_ref, sin_ref, intra_ref, qd_ref, kd_ref, cd_ref, gain_ref,
                      o_ref, state_ref):
    @pl.when(pl.program_id(2) == 0)
    def _():
        state_ref[...] = jnp.zeros_like(state_ref)

    intra, q_decay, k_decay, chunk_decay = intra_ref[0], qd_ref[0], kd_ref[0], cd_ref[0]
    for c in range(RET_ROWS // RET_CHUNK):
        rows = pl.ds(c * RET_CHUNK, RET_CHUNK)
        cos, sin = cos_ref[rows, :], sin_ref[rows, :]
        qc = _rotary(q_ref[0, rows, :].astype(F32), cos, sin) * (RET_QK ** -0.5)
        kc = _rotary(k_ref[0, rows, :].astype(F32), cos, sin)
        vc = v_ref[0, rows, :]
        state = state_ref[...]
        scores = lax.dot_general(qc.astype(BF16), kc.astype(BF16), _NT, preferred_element_type=F32) * intra
        y = _dot(scores.astype(BF16), vc) + _dot((qc * q_decay).astype(BF16), state.astype(BF16))
        state_ref[...] = state * chunk_decay + lax.dot_general(
            (kc * k_decay).astype(BF16), vc, _TN, preferred_element_type=F32)
        mu = jnp.mean(y, axis=-1, keepdims=True)
        yc = y - mu
        var = jnp.mean(yc * yc, axis=-1, keepdims=True)
        yn = yc * lax.rsqrt(var + GN_EPS) * gain_ref[...]
        g = g_ref[0, rows, :].astype(F32)
        o_ref[0, rows, :] = (g / (1.0 + jnp.exp(-g)) * yn).astype(o_ref.dtype)


def _retention(proj, gn_gain, batch, seq):
    cos, sin, intra, q_decay, k_decay, chunk_decay = _retention_tables(seq)
    qk_blocks = D_MODEL // RET_QK
    v_blocks = 2 * D_MODEL // RET_V
    per_head = lambda shape: pl.BlockSpec((1,) + shape, lambda b, h, r: (h, 0, 0))
    return pl.pallas_call(
        _retention_kernel,
        grid=(batch, RET_HEADS, seq // RET_ROWS),
        in_specs=[
            pl.BlockSpec((1, RET_ROWS, RET_QK), lambda b, h, r: (b, r, h)),
            pl.BlockSpec((1, RET_ROWS, RET_QK), lambda b, h, r: (b, r, qk_blocks + h)),
            pl.BlockSpec((1, RET_ROWS, RET_V), lambda b, h, r: (b, r, v_blocks + h)),
            pl.BlockSpec((1, RET_ROWS, RET_V), lambda b, h, r: (b, r, v_blocks + RET_HEADS + h)),
            pl.BlockSpec((RET_ROWS, RET_QK // 2), lambda b, h, r: (r, 0)),
            pl.BlockSpec((RET_ROWS, RET_QK // 2), lambda b, h, r: (r, 0)),
            per_head((RET_CHUNK, RET_CHUNK)),
            per_head((RET_CHUNK, 1)),
            per_head((RET_CHUNK, 1)),
            per_head((1, RET_V)),
            pl.BlockSpec((1, RET_V), lambda b, h, r: (0, h)),
        ],
        out_specs=pl.BlockSpec((1, RET_ROWS, RET_V), lambda b, h, r: (b, r, h)),
        out_shape=jax.ShapeDtypeStruct((batch, seq, RET_HEADS * RET_V), BF16),
        scratch_shapes=[pltpu.VMEM((RET_QK, RET_V), F32)],
        compiler_params=_params("parallel", "parallel", "arbitrary"),
    )(proj, proj, proj, proj, cos, sin, intra, q_decay, k_decay, chunk_decay,
      gn_gain.astype(F32).reshape(1, RET_HEADS * RET_V))


def _sb_mixer(x, norm, w_in, batch, seq):
    qkv = _norm_matmul(x, norm, w_in.astype(BF16), BF16, 1024)
    return _sb_attention(qkv.reshape(batch, seq, -1), batch, seq).reshape(batch * seq, -1)


def _retention_mixer(x, norm, w_in, gn_gain, batch, seq):
    proj = _norm_matmul(x, norm, w_in.astype(BF16), BF16, 1024)
    return _retention(proj.reshape(batch, seq, -1), gn_gain, batch, seq).reshape(batch * seq, -1)


def _fox_mixer(x, norm, w_in, b_forget, batch, seq):
    w_qkv = w_in[:, :3 * D_MODEL].astype(BF16)
    w_f = jnp.pad(w_in[:, 3 * D_MODEL:], ((0, 0), (0, LANES - ATT_HEADS))).astype(BF16)
    qkv, f_logits = _norm_matmul_gate(x, norm, w_qkv, w_f, 1024)
    aug_q, aug_k, cum_last = _fox_cum(f_logits.reshape(batch, seq, LANES), b_forget, batch, seq)
    last_table = cum_last[:, :, 0, :ATT_HEADS].transpose(0, 2, 1).reshape(-1)
    o = _fox_attention(qkv.reshape(batch, seq, -1), aug_q, aug_k, last_table, batch, seq)
    return o.reshape(batch * seq, -1)


def kernel(x, norm_mix_0, w_in_0, w_out_0, norm_mlp_0, w_up_0, w_down_0, norm_mix_1, w_in_1, gn_gain_1, w_out_1, norm_mlp_1, w_up_1, w_down_1, norm_mix_2, w_in_2, b_forget_2, w_out_2, norm_mlp_2, w_up_2, w_down_2, norm_mix_3, w_in_3, w_out_3, norm_mlp_3, w_up_3, w_down_3, norm_final):
    batch, seq, d = x.shape
    h = x.reshape(batch * seq, d)
    tails = ((w_out_0, norm_mlp_0, w_up_0, w_down_0), (w_out_1, norm_mlp_1, w_up_1, w_down_1),
             (w_out_2, norm_mlp_2, w_up_2, w_down_2), (w_out_3, norm_mlp_3, w_up_3, w_down_3))
    for layer in range(4):
        if layer == 0:
            a = _sb_mixer(h, norm_mix_0, w_in_0, batch, seq)
        elif layer == 1:
            a = _retention_mixer(h, norm_mix_1, w_in_1, gn_gain_1, batch, seq)
        elif layer == 2:
            a = _fox_mixer(h, norm_mix_2, w_in_2, b_forget_2, batch, seq)
        else:
            a = _sb_mixer(h, norm_mix_3, w_in_3, batch, seq)
        w_out, norm_mlp, w_up, w_down = tails[layer]
        h = _mixer_out_mlp(a, w_out.astype(BF16), h, norm_mlp, w_up.astype(BF16), w_down.astype(BF16),
                           norm_final, layer == 3)
    return h.reshape(batch, seq, d)
```

```python
import functools

import jax
import jax.numpy as jnp
import numpy as np
from jax import lax
from jax.experimental import pallas as pl
from jax.experimental.pallas import tpu as pltpu

F32 = jnp.float32
BF16 = jnp.bfloat16

D_MODEL = 1024
D_FF = 4 * D_MODEL
HEAD_DIM = 64
ATT_HEADS = D_MODEL // HEAD_DIM
LANES = 128
HEAD_PAIRS = D_MODEL // LANES
RET_HEADS = 4
RET_QK = D_MODEL // RET_HEADS
RET_V = 2 * RET_QK
RET_CHUNK = 128
ROPE_BASE = 10000.0
NORM_EPS = 1e-6
GN_EPS = 1e-5

ROW_TILE = 1024
ATT_TILE = 256
STEP_ROWS = 2 * ATT_TILE
RET_ROWS = 512
SKIP_BOUND = 105.0
VMEM_LIMIT = 56 * 1024 * 1024

_NT = (((1,), (1,)), ((), ()))
_TN = (((0,), (0,)), ((), ()))


def _params(*semantics):
    return pltpu.CompilerParams(dimension_semantics=semantics, vmem_limit_bytes=VMEM_LIMIT)


def _dot(a, b):
    return jnp.dot(a, b, preferred_element_type=F32)


def _rms_normed(x, gain):
    return x * lax.rsqrt(jnp.mean(x * x, axis=-1, keepdims=True) + NORM_EPS) * gain


def _softplus(x):
    return jnp.maximum(x, 0.0) + jnp.log(1.0 + jnp.exp(-jnp.abs(x)))


def _norm_matmul_kernel(x_ref, g_ref, w_ref, o_ref, h_ref):
    @pl.when(pl.program_id(1) == 0)
    def _():
        h_ref[...] = _rms_normed(x_ref[...], g_ref[...]).astype(BF16)

    o_ref[...] = _dot(h_ref[...], w_ref[...]).astype(o_ref.dtype)


def _norm_matmul(x, gain, w, out_dtype, col_tile):
    t, d = x.shape
    n = w.shape[1]
    return pl.pallas_call(
        _norm_matmul_kernel,
        grid=(t // ROW_TILE, n // col_tile),
        in_specs=[
            pl.BlockSpec((ROW_TILE, d), lambda i, j: (i, 0)),
            pl.BlockSpec((1, d), lambda i, j: (0, 0)),
            pl.BlockSpec((d, col_tile), lambda i, j: (0, j)),
        ],
        out_specs=pl.BlockSpec((ROW_TILE, col_tile), lambda i, j: (i, j)),
        out_shape=jax.ShapeDtypeStruct((t, n), out_dtype),
        scratch_shapes=[pltpu.VMEM((ROW_TILE, d), BF16)],
        compiler_params=_params("parallel", "arbitrary"),
    )(x, gain.reshape(1, d), w)


def _norm_matmul_gate_kernel(x_ref, g_ref, w_ref, wf_ref, o_ref, f_ref, h_ref):
    @pl.when(pl.program_id(1) == 0)
    def _():
        h_ref[...] = _rms_normed(x_ref[...], g_ref[...]).astype(BF16)
        f_ref[...] = _dot(h_ref[...], wf_ref[...])

    o_ref[...] = _dot(h_ref[...], w_ref[...]).astype(o_ref.dtype)


def _norm_matmul_gate(x, gain, w, w_gate, col_tile):
    t, d = x.shape
    n = w.shape[1]
    return pl.pallas_call(
        _norm_matmul_gate_kernel,
        grid=(t // ROW_TILE, n // col_tile),
        in_specs=[
            pl.BlockSpec((ROW_TILE, d), lambda i, j: (i, 0)),
            pl.BlockSpec((1, d), lambda i, j: (0, 0)),
            pl.BlockSpec((d, col_tile), lambda i, j: (0, j)),
            pl.BlockSpec((d, LANES), lambda i, j: (0, 0)),
        ],
        out_specs=[pl.BlockSpec((ROW_TILE, col_tile), lambda i, j: (i, j)),
                   pl.BlockSpec((ROW_TILE, LANES), lambda i, j: (i, 0))],
        out_shape=[jax.ShapeDtypeStruct((t, n), BF16), jax.ShapeDtypeStruct((t, LANES), F32)],
        scratch_shapes=[pltpu.VMEM((ROW_TILE, d), BF16)],
        compiler_params=_params("parallel", "arbitrary"),
    )(x, gain.reshape(1, d), w, w_gate)


def _mixer_out_mlp_kernel(a_ref, wo_ref, x_ref, g_ref, wu_ref, wd_ref, gf_ref, o_ref, h_ref, *, final_norm):
    f = pl.program_id(1)

    @pl.when(f == 0)
    def _():
        o_ref[...] = x_ref[...] + _dot(a_ref[...], wo_ref[...])
        h_ref[...] = _rms_normed(o_ref[...], g_ref[...]).astype(BF16)

    u = jnp.maximum(_dot(h_ref[...], wu_ref[...]), 0.0)
    o_ref[...] += _dot((u * u).astype(BF16), wd_ref[...])

    if final_norm:
        @pl.when(f == pl.num_programs(1) - 1)
        def _():
            o_ref[...] = _rms_normed(o_ref[...], gf_ref[...])


def _mixer_out_mlp(a, w_out, x, gain, w_up, w_down, final_gain, final_norm):
    t, d = x.shape
    k = a.shape[1]
    ff = w_up.shape[1]
    ff_tile = 1024
    return pl.pallas_call(
        functools.partial(_mixer_out_mlp_kernel, final_norm=final_norm),
        grid=(t // ROW_TILE, ff // ff_tile),
        in_specs=[
            pl.BlockSpec((ROW_TILE, k), lambda i, f: (i, 0)),
            pl.BlockSpec((k, d), lambda i, f: (0, 0)),
            pl.BlockSpec((ROW_TILE, d), lambda i, f: (i, 0)),
            pl.BlockSpec((1, d), lambda i, f: (0, 0)),
            pl.BlockSpec((d, ff_tile), lambda i, f: (0, f)),
            pl.BlockSpec((ff_tile, d), lambda i, f: (f, 0)),
            pl.BlockSpec((1, d), lambda i, f: (0, 0)),
        ],
        out_specs=pl.BlockSpec((ROW_TILE, d), lambda i, f: (i, 0)),
        out_shape=jax.ShapeDtypeStruct((t, d), F32),
        scratch_shapes=[pltpu.VMEM((ROW_TILE, d), BF16)],
        compiler_params=_params("parallel", "arbitrary"),
    )(a, w_out, x, gain.reshape(1, d), w_up, w_down, final_gain.reshape(1, d))


def _head_pair_masks():
    lane = lax.broadcasted_iota(jnp.int32, (1, LANES), 1)
    return lane < HEAD_DIM, lane >= HEAD_DIM


def _suffix_ones():
    idx = np.arange(ATT_TILE)
    return jnp.asarray(idx[:, None] >= idx[None, :], BF16)


def _stack_heads(x_heads, tiles):
    return jnp.concatenate([x_heads[h][t * ATT_TILE:(t + 1) * ATT_TILE]
                            for t in range(tiles) for h in range(2)], axis=0)


def _unstack_heads(stacked, first, tiles):
    parts = [jnp.where(first, stacked[(2 * t) * ATT_TILE:(2 * t + 1) * ATT_TILE],
                       stacked[(2 * t + 1) * ATT_TILE:(2 * t + 2) * ATT_TILE])
             for t in range(tiles)]
    return parts[0] if tiles == 1 else jnp.concatenate(parts, axis=0)


TILE_ROWS = 2 * ATT_TILE
STACK_ROWS = 2 * STEP_ROWS


def _stacked_diag_mask(n_rows, strict, later_tiles_open):
    row = lax.broadcasted_iota(jnp.int32, (n_rows, ATT_TILE), 0)
    col = lax.broadcasted_iota(jnp.int32, (n_rows, ATT_TILE), 1)
    pos = row & (ATT_TILE - 1)
    causal = (col < pos) if strict else (col <= pos)
    return jnp.logical_or(causal, row >= TILE_ROWS) if later_tiles_open else causal


def _sb_kernel(q_ref, k_ref, v_ref, tri_ref, o_ref, acc_ref, carry_ref):
    i = pl.program_id(2)
    first, second = _head_pair_masks()
    q = q_ref[0] * (HEAD_DIM ** -0.5)
    zero = jnp.zeros_like(q)
    q_stack = _stack_heads((jnp.where(first, q, zero), jnp.where(second, q, zero)), STEP_ROWS // ATT_TILE)
    tri = tri_ref[...]

    def front(row0, n_rows, j, mask):
        start = pl.multiple_of(j * ATT_TILE, ATT_TILE)
        k = k_ref[0, pl.ds(start, ATT_TILE), :]
        z = lax.dot_general(q_stack[row0:row0 + n_rows], k, _NT, preferred_element_type=F32)
        sp = _softplus(z)
        if mask is not None:
            sp = jnp.where(mask, sp, 0.0)
        hi = sp.astype(BF16)
        lo = (sp - hi.astype(F32)).astype(BF16)
        suffix = _dot(hi, tri) + _dot(lo, tri)
        return z - suffix, suffix[:, 0:1], v_ref[0, pl.ds(start, ATT_TILE), :]

    def back(exponent, carry, mask, v):
        w = jnp.exp(exponent - carry)
        if mask is not None:
            w = jnp.where(mask, w, 0.0)
        return _dot(w.astype(BF16), v)

    def visit(row0, n_rows, j):
        rows = pl.ds(row0, n_rows)
        exponent, total, v = front(row0, n_rows, j, None)
        acc_ref[rows, :] += back(exponent, carry_ref[rows, :], None, v)
        carry_ref[rows, :] += total

    first_block = (STEP_ROWS // ATT_TILE) * i
    has_previous = (i > 0).astype(F32)
    mask_late = _stacked_diag_mask(TILE_ROWS, True, False)
    mask_all = _stacked_diag_mask(STACK_ROWS, True, True)
    exp_late, total_late, v_late = front(TILE_ROWS, TILE_ROWS, first_block + 1, mask_late)
    exp_all, total_all, v_all = front(0, STACK_ROWS, first_block, mask_all)
    exp_prev, total_prev, v_prev = front(0, TILE_ROWS, jnp.maximum(first_block - 1, 0), None)
    no_carry = jnp.zeros((TILE_ROWS, 1), F32)
    pv_late = back(exp_late, no_carry, mask_late, v_late)
    pv_all = back(exp_all, jnp.concatenate([no_carry, total_late], axis=0), mask_all, v_all)
    pv_prev = back(exp_prev, total_all[:TILE_ROWS], None, v_prev) * has_previous
    acc_ref[...] = pv_all + jnp.concatenate([pv_prev, pv_late], axis=0)
    carry_ref[...] = total_all + jnp.concatenate([total_prev * has_previous, total_late], axis=0)

    def visit_earlier(tile):
        row0 = tile * TILE_ROWS

        def exhausted():
            return jnp.min(carry_ref[pl.ds(row0, TILE_ROWS), :]) > SKIP_BOUND

        def cond(state):
            j, done = state
            return jnp.logical_and(j >= 0, jnp.logical_not(done))

        def body(state):
            j, _ = state
            visit(row0, TILE_ROWS, j)
            return j - 1, exhausted()

        lax.while_loop(cond, body, (first_block + tile - 2, exhausted()))

    for tile in range(STEP_ROWS // ATT_TILE):
        visit_earlier(tile)
    o_ref[0] = _unstack_heads(acc_ref[...], first, STEP_ROWS // ATT_TILE).astype(o_ref.dtype)


def _sb_attention(qkv, batch, seq):
    return pl.pallas_call(
        _sb_kernel,
        grid=(batch, HEAD_PAIRS, seq // STEP_ROWS),
        in_specs=[
            pl.BlockSpec((1, STEP_ROWS, LANES), lambda b, p, i: (b, i, p)),
            pl.BlockSpec((1, seq, LANES), lambda b, p, i: (b, 0, HEAD_PAIRS + p)),
            pl.BlockSpec((1, seq, LANES), lambda b, p, i: (b, 0, 2 * HEAD_PAIRS + p)),
            pl.BlockSpec((ATT_TILE, ATT_TILE), lambda b, p, i: (0, 0)),
        ],
        out_specs=pl.BlockSpec((1, STEP_ROWS, LANES), lambda b, p, i: (b, i, p)),
        out_shape=jax.ShapeDtypeStruct((batch, seq, D_MODEL), BF16),
        scratch_shapes=[pltpu.VMEM((STACK_ROWS, LANES), F32), pltpu.VMEM((STACK_ROWS, 1), F32)],
        compiler_params=_params("parallel", "parallel", "arbitrary"),
    )(qkv, qkv, qkv, _suffix_ones())


def _split3(x):
    def top(v):
        return pltpu.bitcast(pltpu.bitcast(v, jnp.uint32) & jnp.uint32(0xFFFF0000), F32)
    a = top(x)
    r = x - a
    b = top(r)
    return a.astype(BF16), b.astype(BF16), (r - b).astype(BF16)


AUG_LANES = 6


def _fox_selectors():
    sel_q = np.zeros((HEAD_PAIRS, 3 * LANES, LANES), np.float32)
    sel_k = np.zeros((HEAD_PAIRS, 3 * LANES, LANES), np.float32)
    const_q = np.zeros((1, LANES), np.float32)
    const_k = np.zeros((1, LANES), np.float32)
    for m in range(2):
        for part in range(3):
            const_q[0, AUG_LANES * m + 3 + part] = 1.0
            const_k[0, AUG_LANES * m + part] = 1.0
            for p in range(HEAD_PAIRS):
                head = 2 * p + m
                sel_q[p, part * LANES + head, AUG_LANES * m + part] = 1.0
                sel_k[p, part * LANES + head, AUG_LANES * m + 3 + part] = -1.0
    return (jnp.asarray(sel_q, BF16), jnp.asarray(sel_k, BF16), jnp.asarray(const_q), jnp.asarray(const_k))


def _fox_cum_kernel(f_ref, b_ref, sq_ref, sk_ref, cq_ref, ck_ref, aq_ref, ak_ref, last_ref, carry_ref):
    @pl.when(pl.program_id(1) == 0)
    def _():
        carry_ref[...] = jnp.zeros_like(carry_ref)

    log_f = -_softplus(-(f_ref[0] + b_ref[...]))
    row = lax.broadcasted_iota(jnp.int32, (ATT_TILE, ATT_TILE), 0)
    col = lax.broadcasted_iota(jnp.int32, (ATT_TILE, ATT_TILE), 1)
    prefix_ones = (row >= col).astype(BF16)
    parts = _split3(log_f)
    cum = carry_ref[...] + _dot(prefix_ones, parts[0]) + _dot(prefix_ones, parts[1]) + _dot(prefix_ones, parts[2])
    carry_ref[...] = cum[ATT_TILE - 1:ATT_TILE, :]
    last_ref[0, 0] = cum[ATT_TILE - 1:ATT_TILE, :]
    stacked = jnp.concatenate(_split3(cum), axis=-1)
    for p in range(HEAD_PAIRS):
        aq_ref[0, p] = (_dot(stacked, sq_ref[p]) + cq_ref[...]).astype(BF16)
        ak_ref[0, p] = (_dot(stacked, sk_ref[p]) + ck_ref[...]).astype(BF16)


def _fox_cum(f_logits, bias, batch, seq):
    n_t = seq // ATT_TILE
    sel_q, sel_k, const_q, const_k = _fox_selectors()
    bias_row = jnp.zeros((1, LANES), F32).at[0, :ATT_HEADS].set(bias.astype(F32))
    whole = lambda shape: pl.BlockSpec(shape, lambda b, r: (0,) * len(shape))
    return pl.pallas_call(
        _fox_cum_kernel,
        grid=(batch, n_t),
        in_specs=[
            pl.BlockSpec((1, ATT_TILE, LANES), lambda b, r: (b, r, 0)),
            whole((1, LANES)),
            whole(sel_q.shape), whole(sel_k.shape), whole((1, LANES)), whole((1, LANES)),
        ],
        out_specs=[
            pl.BlockSpec((1, HEAD_PAIRS, ATT_TILE, LANES), lambda b, r: (b, 0, r, 0)),
            pl.BlockSpec((1, HEAD_PAIRS, ATT_TILE, LANES), lambda b, r: (b, 0, r, 0)),
            pl.BlockSpec((1, 1, 1, LANES), lambda b, r: (b, r, 0, 0)),
        ],
        out_shape=[
            jax.ShapeDtypeStruct((batch, HEAD_PAIRS, seq, LANES), BF16),
            jax.ShapeDtypeStruct((batch, HEAD_PAIRS, seq, LANES), BF16),
            jax.ShapeDtypeStruct((batch, n_t, 1, LANES), F32),
        ],
        scratch_shapes=[pltpu.VMEM((1, LANES), F32)],
        compiler_params=_params("parallel", "arbitrary"),
    )(f_logits, bias_row, sel_q, sel_k, const_q, const_k)


KNORM_ROWS = 512


def _fox_kernel(last_ref, q_ref, aq_ref, k_ref, ak_ref, v_ref, o_ref, acc_ref, m_ref, l_ref, knorm_ref,
                *, n_blocks):
    b, p, i = pl.program_id(0), pl.program_id(1), pl.program_id(2)
    first, second = _head_pair_masks()
    head_lanes = (first, second)
    seq = k_ref.shape[1]

    @pl.when(i == 0)
    def _():
        def scan(c, best):
            kk = k_ref[0, pl.ds(pl.multiple_of(c * KNORM_ROWS, KNORM_ROWS), KNORM_ROWS), :].astype(F32)
            sq = kk * kk
            return tuple(jnp.maximum(best[h], jnp.sum(jnp.where(head_lanes[h], sq, 0.0), axis=-1, keepdims=True))
                         for h in range(2))
        init = (jnp.zeros((KNORM_ROWS, 1), F32),) * 2
        best = lax.fori_loop(0, seq // KNORM_ROWS, scan, init)
        for h in range(2):
            knorm_ref[h] = jnp.max(jnp.sqrt(best[h]))

    q = q_ref[0] * (HEAD_DIM ** -0.5)
    aq = aq_ref[0, 0]
    zero = jnp.zeros_like(q)
    lane = lax.broadcasted_iota(jnp.int32, (1, LANES), 1)
    aqf = aq.astype(F32)
    qf = q.astype(F32)
    q_heads, cum_t, q_norm = [], [], []
    for h in range(2):
        aug_lanes = jnp.logical_and(lane >= AUG_LANES * h, lane < AUG_LANES * (h + 1))
        q_heads.append(jnp.concatenate([jnp.where(head_lanes[h], q, zero), jnp.where(aug_lanes, aq, zero)], axis=-1))
        o = AUG_LANES * h
        cum_t.append(aqf[:, o:o + 1] + aqf[:, o + 1:o + 2] + aqf[:, o + 2:o + 3])
        q_norm.append(jnp.sqrt(jnp.sum(jnp.where(head_lanes[h], qf * qf, 0.0), axis=-1, keepdims=True)))
    row = lax.broadcasted_iota(jnp.int32, (ATT_TILE, ATT_TILE), 0)
    col = lax.broadcasted_iota(jnp.int32, (ATT_TILE, ATT_TILE), 1)
    causal = col <= row

    def local_softmax(h, j, mask, present):
        start = pl.multiple_of(jnp.maximum(j, 0) * ATT_TILE, ATT_TILE)
        kk = jnp.concatenate([k_ref[0, pl.ds(start, ATT_TILE), :], ak_ref[0, 0, pl.ds(start, ATT_TILE), :]], axis=-1)
        v = v_ref[0, pl.ds(start, ATT_TILE), :]
        s = lax.dot_general(q_heads[h], kk, _NT, preferred_element_type=F32)
        if mask is not None:
            s = jnp.where(mask, s, -1e30)
        if present is not None:
            s = jnp.where(present, s, -1e30)
        m = jnp.broadcast_to(jnp.max(s, axis=-1, keepdims=True), (ATT_TILE, LANES))
        prob = jnp.exp(s - jnp.concatenate([m, m], axis=-1))
        total = jnp.broadcast_to(jnp.sum(prob, axis=-1, keepdims=True), (ATT_TILE, LANES))
        return m, total, _dot(prob.astype(BF16), v)

    def merge(h, state, parts):
        m_old, l_old, acc_old = state
        m_new = functools.reduce(jnp.maximum, [part[0] for part in parts], m_old)
        scale = jnp.exp(m_old - m_new)
        l_new, acc_new = scale * l_old, scale * acc_old
        for m, total, pv in parts:
            scale = jnp.exp(m - m_new)
            l_new, acc_new = l_new + scale * total, acc_new + scale * pv
        m_ref[h], l_ref[h], acc_ref[h] = m_new, l_new, acc_new

    empty = (jnp.full((ATT_TILE, LANES), -1e30, F32), jnp.zeros((ATT_TILE, LANES), F32),
             jnp.zeros((ATT_TILE, LANES), F32))
    for h in range(2):
        merge(h, empty, [local_softmax(h, i, causal, None), local_softmax(h, i - 1, None, i > 0)])

    bound = [jnp.max(q_norm[h] * knorm_ref[h] + cum_t[h] - m_ref[h]) for h in range(2)]
    base = [(b * ATT_HEADS + 2 * p + h) * n_blocks for h in range(2)]

    def cond(j):
        jj = jnp.maximum(j, 0)
        live = jnp.logical_or(bound[0] - last_ref[base[0] + jj] >= -SKIP_BOUND,
                              bound[1] - last_ref[base[1] + jj] >= -SKIP_BOUND)
        return jnp.logical_and(j >= 0, live)

    def body(j):
        parts = [[local_softmax(h, j, None, None), local_softmax(h, j - 1, None, j > 0)] for h in range(2)]
        states = [(m_ref[h], l_ref[h], acc_ref[h]) for h in range(2)]
        for h in range(2):
            merge(h, states[h], parts[h])
        return j - 2

    lax.while_loop(cond, body, i - 2)
    o_ref[0] = jnp.where(first, acc_ref[0] / l_ref[0], acc_ref[1] / l_ref[1]).astype(o_ref.dtype)


def _fox_attention(qkv, aug_q, aug_k, cum_last, batch, seq):
    grid_spec = pltpu.PrefetchScalarGridSpec(
        num_scalar_prefetch=1,
        grid=(batch, HEAD_PAIRS, seq // ATT_TILE),
        in_specs=[
            pl.BlockSpec((1, ATT_TILE, LANES), lambda b, p, i, t: (b, i, p)),
            pl.BlockSpec((1, 1, ATT_TILE, LANES), lambda b, p, i, t: (b, p, i, 0)),
            pl.BlockSpec((1, seq, LANES), lambda b, p, i, t: (b, 0, HEAD_PAIRS + p)),
            pl.BlockSpec((1, 1, seq, LANES), lambda b, p, i, t: (b, p, 0, 0)),
            pl.BlockSpec((1, seq, LANES), lambda b, p, i, t: (b, 0, 2 * HEAD_PAIRS + p)),
        ],
        out_specs=pl.BlockSpec((1, ATT_TILE, LANES), lambda b, p, i, t: (b, i, p)),
        scratch_shapes=[
            pltpu.VMEM((2, ATT_TILE, LANES), F32),
            pltpu.VMEM((2, ATT_TILE, LANES), F32),
            pltpu.VMEM((2, ATT_TILE, LANES), F32),
            pltpu.SMEM((2,), F32),
        ],
    )
    return pl.pallas_call(
        functools.partial(_fox_kernel, n_blocks=seq // ATT_TILE),
        grid_spec=grid_spec,
        out_shape=jax.ShapeDtypeStruct((batch, seq, D_MODEL), BF16),
        compiler_params=_params("parallel", "parallel", "arbitrary"),
    )(cum_last, qkv, aug_q, qkv, aug_k, qkv)


def _retention_tables(seq):
    half = RET_QK // 2
    inv_freq = ROPE_BASE ** (-2.0 * jnp.arange(half, dtype=F32) / RET_QK)
    ang = jnp.arange(seq)[:, None].astype(F32) * inv_freq[None, :]
    log_gamma = jnp.log1p(-jnp.exp2(-5.0 - jnp.arange(RET_HEADS, dtype=F32)))
    idx = jnp.arange(RET_CHUNK, dtype=F32)
    diff = idx[:, None] - idx[None, :]
    intra = jnp.where(diff >= 0, jnp.exp(log_gamma[:, None, None] * jnp.maximum(diff, 0.0)), 0.0)
    q_decay = jnp.exp(log_gamma[:, None] * (idx + 1.0))[:, :, None]
    k_decay = jnp.exp(log_gamma[:, None] * (RET_CHUNK - 1.0 - idx))[:, :, None]
    chunk_decay = jnp.broadcast_to(jnp.exp(log_gamma * RET_CHUNK)[:, None, None], (RET_HEADS, 1, RET_V))
    return jnp.cos(ang), jnp.sin(ang), intra, q_decay, k_decay, chunk_decay


def _rotary(x, cos, sin):
    half = RET_QK // 2
    x1, x2 = x[:, :half], x[:, half:]
    return jnp.concatenate([x1 * cos - x2 * sin, x1 * sin + x2 * cos], axis=-1)


def _retention_kernel(q_ref, k_ref, v_ref, g_ref, cos_ref, sin_ref, intra_ref, qd_ref, kd_ref, cd_ref, gain_ref,
                      o_ref, state_ref):
    @pl.when(pl.program_id(2) == 0)
    def _():
        state_ref[...] = jnp.zeros_like(state_ref)

    intra, q_decay, k_decay, chunk_decay = intra_ref[0], qd_ref[0], kd_ref[0], cd_ref[0]
    for c in range(RET_ROWS // RET_CHUNK):
        rows = pl.ds(c * RET_CHUNK, RET_CHUNK)
        cos, sin = cos_ref[rows, :], sin_ref[rows, :]
        qc = _rotary(q_ref[0, rows, :].astype(F32), cos, sin) * (RET_QK ** -0.5)
        kc = _rotary(k_ref[0, rows, :].astype(F32), cos, sin)
        vc = v_ref[0, rows, :]
        state = state_ref[...]
        scores = lax.dot_general(qc.astype(BF16), kc.astype(BF16), _NT, preferred_element_type=F32) * intra
        y = _dot(scores.astype(BF16), vc) + _dot((qc * q_decay).astype(BF16), state.astype(BF16))
        state_ref[...] = state * chunk_decay + lax.dot_general(
            (kc * k_decay).astype(BF16), vc, _TN, preferred_element_type=F32)
        mu = jnp.mean(y, axis=-1, keepdims=True)
        yc = y - mu
        var = jnp.mean(yc * yc, axis=-1, keepdims=True)
        yn = yc * lax.rsqrt(var + GN_EPS) * gain_ref[...]
        g = g_ref[0, rows, :].astype(F32)
        o_ref[0, rows, :] = (g / (1.0 + jnp.exp(-g)) * yn).astype(o_ref.dtype)


def _retention(proj, gn_gain, batch, seq):
    cos, sin, intra, q_decay, k_decay, chunk_decay = _retention_tables(seq)
    qk_blocks = D_MODEL // RET_QK
    v_blocks = 2 * D_MODEL // RET_V
    per_head = lambda shape: pl.BlockSpec((1,) + shape, lambda b, h, r: (h, 0, 0))
    return pl.pallas_call(
        _retention_kernel,
        grid=(batch, RET_HEADS, seq // RET_ROWS),
        in_specs=[
            pl.BlockSpec((1, RET_ROWS, RET_QK), lambda b, h, r: (b, r, h)),
            pl.BlockSpec((1, RET_ROWS, RET_QK), lambda b, h, r: (b, r, qk_blocks + h)),
            pl.BlockSpec((1, RET_ROWS, RET_V), lambda b, h, r: (b, r, v_blocks + h)),
            pl.BlockSpec((1, RET_ROWS, RET_V), lambda b, h, r: (b, r, v_blocks + RET_HEADS + h)),
            pl.BlockSpec((RET_ROWS, RET_QK // 2), lambda b, h, r: (r, 0)),
            pl.BlockSpec((RET_ROWS, RET_QK // 2), lambda b, h, r: (r, 0)),
            per_head((RET_CHUNK, RET_CHUNK)),
            per_head((RET_CHUNK, 1)),
            per_head((RET_CHUNK, 1)),
            per_head((1, RET_V)),
            pl.BlockSpec((1, RET_V), lambda b, h, r: (0, h)),
        ],
        out_specs=pl.BlockSpec((1, RET_ROWS, RET_V), lambda b, h, r: (b, r, h)),
        out_shape=jax.ShapeDtypeStruct((batch, seq, RET_HEADS * RET_V), BF16),
        scratch_shapes=[pltpu.VMEM((RET_QK, RET_V), F32)],
        compiler_params=_params("parallel", "parallel", "arbitrary"),
    )(proj, proj, proj, proj, cos, sin, intra, q_decay, k_decay, chunk_decay,
      gn_gain.astype(F32).reshape(1, RET_HEADS * RET_V))


def _sb_mixer(x, norm, w_in, batch, seq):
    qkv = _norm_matmul(x, norm, w_in.astype(BF16), BF16, 1024)
    return _sb_attention(qkv.reshape(batch, seq, -1), batch, seq).reshape(batch * seq, -1)


def _retention_mixer(x, norm, w_in, gn_gain, batch, seq):
    proj = _norm_matmul(x, norm, w_in.astype(BF16), BF16, 1024)
    return _retention(proj.reshape(batch, seq, -1), gn_gain, batch, seq).reshape(batch * seq, -1)


def _fox_mixer(x, norm, w_in, b_forget, batch, seq):
    w_qkv = w_in[:, :3 * D_MODEL].astype(BF16)
    w_f = jnp.pad(w_in[:, 3 * D_MODEL:], ((0, 0), (0, LANES - ATT_HEADS))).astype(BF16)
    qkv, f_logits = _norm_matmul_gate(x, norm, w_qkv, w_f, 1024)
    aug_q, aug_k, cum_last = _fox_cum(f_logits.reshape(batch, seq, LANES), b_forget, batch, seq)
    last_table = cum_last[:, :, 0, :ATT_HEADS].transpose(0, 2, 1).reshape(-1)
    o = _fox_attention(qkv.reshape(batch, seq, -1), aug_q, aug_k, last_table, batch, seq)
    return o.reshape(batch * seq, -1)


def kernel(x, norm_mix_0, w_in_0, w_out_0, norm_mlp_0, w_up_0, w_down_0, norm_mix_1, w_in_1, gn_gain_1, w_out_1, norm_mlp_1, w_up_1, w_down_1, norm_mix_2, w_in_2, b_forget_2, w_out_2, norm_mlp_2, w_up_2, w_down_2, norm_mix_3, w_in_3, w_out_3, norm_mlp_3, w_up_3, w_down_3, norm_final):
    batch, seq, d = x.shape
    h = x.reshape(batch * seq, d)
    tails = ((w_out_0, norm_mlp_0, w_up_0, w_down_0), (w_out_1, norm_mlp_1, w_up_1, w_down_1),
             (w_out_2, norm_mlp_2, w_up_2, w_down_2), (w_out_3, norm_mlp_3, w_up_3, w_down_3))
    for layer in range(4):
        if layer == 0:
            a = _sb_mixer(h, norm_mix_0, w_in_0, batch, seq)
        elif layer == 1:
            a = _retention_mixer(h, norm_mix_1, w_in_1, gn_gain_1, batch, seq)
        elif layer == 2:
            a = _fox_mixer(h, norm_mix_2, w_in_2, b_forget_2, batch, seq)
        else:
            a = _sb_mixer(h, norm_mix_3, w_in_3, batch, seq)
        w_out, norm_mlp, w_up, w_down = tails[layer]
        h = _mixer_out_mlp(a, w_out.astype(BF16), h, norm_mlp, w_up.astype(BF16), w_down.astype(BF16),
                           norm_final, layer == 3)
    return h.reshape(batch, seq, d)
```

```python
import functools

import jax
import jax.numpy as jnp
import numpy as np
from jax import lax
from jax.experimental import pallas as pl
from jax.experimental.pallas import tpu as pltpu

F32 = jnp.float32
BF16 = jnp.bfloat16

D_MODEL = 1024
D_FF = 4 * D_MODEL
HEAD_DIM = 64
ATT_HEADS = D_MODEL // HEAD_DIM
LANES = 128
HEAD_PAIRS = D_MODEL // LANES
RET_HEADS = 4
RET_QK = D_MODEL // RET_HEADS
RET_V = 2 * RET_QK
RET_CHUNK = 128
ROPE_BASE = 10000.0
NORM_EPS = 1e-6
GN_EPS = 1e-5

ROW_TILE = 1024
ATT_TILE = 256
STEP_ROWS = 2 * ATT_TILE
RET_ROWS = 512
SKIP_BOUND = 105.0
VMEM_LIMIT = 56 * 1024 * 1024

_NT = (((1,), (1,)), ((), ()))
_TN = (((0,), (0,)), ((), ()))


def _params(*semantics):
    return pltpu.CompilerParams(dimension_semantics=semantics, vmem_limit_bytes=VMEM_LIMIT)


def _dot(a, b):
    return jnp.dot(a, b, preferred_element_type=F32)


def _rms_normed(x, gain):
    return x * lax.rsqrt(jnp.mean(x * x, axis=-1, keepdims=True) + NORM_EPS) * gain


def _softplus(x):
    return jnp.maximum(x, 0.0) + jnp.log(1.0 + jnp.exp(-jnp.abs(x)))


def _norm_matmul_kernel(x_ref, g_ref, w_ref, o_ref, h_ref):
    @pl.when(pl.program_id(1) == 0)
    def _():
        h_ref[...] = _rms_normed(x_ref[...], g_ref[...]).astype(BF16)

    o_ref[...] = _dot(h_ref[...], w_ref[...]).astype(o_ref.dtype)


def _norm_matmul(x, gain, w, out_dtype, col_tile):
    t, d = x.shape
    n = w.shape[1]
    return pl.pallas_call(
        _norm_matmul_kernel,
        grid=(t // ROW_TILE, n // col_tile),
        in_specs=[
            pl.BlockSpec((ROW_TILE, d), lambda i, j: (i, 0)),
            pl.BlockSpec((1, d), lambda i, j: (0, 0)),
            pl.BlockSpec((d, col_tile), lambda i, j: (0, j)),
        ],
        out_specs=pl.BlockSpec((ROW_TILE, col_tile), lambda i, j: (i, j)),
        out_shape=jax.ShapeDtypeStruct((t, n), out_dtype),
        scratch_shapes=[pltpu.VMEM((ROW_TILE, d), BF16)],
        compiler_params=_params("parallel", "arbitrary"),
    )(x, gain.reshape(1, d), w)


def _norm_matmul_gate_kernel(x_ref, g_ref, w_ref, wf_ref, o_ref, f_ref, h_ref):
    @pl.when(pl.program_id(1) == 0)
    def _():
        h_ref[...] = _rms_normed(x_ref[...], g_ref[...]).astype(BF16)
        f_ref[...] = _dot(h_ref[...], wf_ref[...])

    o_ref[...] = _dot(h_ref[...], w_ref[...]).astype(o_ref.dtype)


def _norm_matmul_gate(x, gain, w, w_gate, col_tile):
    t, d = x.shape
    n = w.shape[1]
    return pl.pallas_call(
        _norm_matmul_gate_kernel,
        grid=(t // ROW_TILE, n // col_tile),
        in_specs=[
            pl.BlockSpec((ROW_TILE, d), lambda i, j: (i, 0)),
            pl.BlockSpec((1, d), lambda i, j: (0, 0)),
            pl.BlockSpec((d, col_tile), lambda i, j: (0, j)),
            pl.BlockSpec((d, LANES), lambda i, j: (0, 0)),
        ],
        out_specs=[pl.BlockSpec((ROW_TILE, col_tile), lambda i, j: (i, j)),
                   pl.BlockSpec((ROW_TILE, LANES), lambda i, j: (i, 0))],
        out_shape=[jax.ShapeDtypeStruct((t, n), BF16), jax.ShapeDtypeStruct((t, LANES), F32)],
        scratch_shapes=[pltpu.VMEM((ROW_TILE, d), BF16)],
        compiler_params=_params("parallel", "arbitrary"),
    )(x, gain.reshape(1, d), w, w_gate)


def _mixer_out_mlp_kernel(a_ref, wo_ref, x_ref, g_ref, wu_ref, wd_ref, gf_ref, o_ref, h_ref, *, final_norm):
    f = pl.program_id(1)

    @pl.when(f == 0)
    def _():
        o_ref[...] = x_ref[...] + _dot(a_ref[...], wo_ref[...])
        h_ref[...] = _rms_normed(o_ref[...], g_ref[...]).astype(BF16)

    u = jnp.maximum(_dot(h_ref[...], wu_ref[...]), 0.0)
    o_ref[...] += _dot((u * u).astype(BF16), wd_ref[...])

    if final_norm:
        @pl.when(f == pl.num_programs(1) - 1)
        def _():
            o_ref[...] = _rms_normed(o_ref[...], gf_ref[...])


def _mixer_out_mlp(a, w_out, x, gain, w_up, w_down, final_gain, final_norm):
    t, d = x.shape
    k = a.shape[1]
    ff = w_up.shape[1]
    ff_tile = 1024
    return pl.pallas_call(
        functools.partial(_mixer_out_mlp_kernel, final_norm=final_norm),
        grid=(t // ROW_TILE, ff // ff_tile),
        in_specs=[
            pl.BlockSpec((ROW_TILE, k), lambda i, f: (i, 0)),
            pl.BlockSpec((k, d), lambda i, f: (0, 0)),
            pl.BlockSpec((ROW_TILE, d), lambda i, f: (i, 0)),
            pl.BlockSpec((1, d), lambda i, f: (0, 0)),
            pl.BlockSpec((d, ff_tile), lambda i, f: (0, f)),
            pl.BlockSpec((ff_tile, d), lambda i, f: (f, 0)),
            pl.BlockSpec((1, d), lambda i, f: (0, 0)),
        ],
        out_specs=pl.BlockSpec((ROW_TILE, d), lambda i, f: (i, 0)),
        out_shape=jax.ShapeDtypeStruct((t, d), F32),
        scratch_shapes=[pltpu.VMEM((ROW_TILE, d), BF16)],
        compiler_params=_params("parallel", "arbitrary"),
    )(a, w_out, x, gain.reshape(1, d), w_up, w_down, final_gain.reshape(1, d))


def _head_pair_masks():
    lane = lax.broadcasted_iota(jnp.int32, (1, LANES), 1)
    return lane < HEAD_DIM, lane >= HEAD_DIM


def _suffix_ones():
    idx = np.arange(ATT_TILE)
    return jnp.asarray(idx[:, None] >= idx[None, :], BF16)


def _stack_heads(x_heads, tiles):
    return jnp.concatenate([x_heads[h][t * ATT_TILE:(t + 1) * ATT_TILE]
                            for t in range(tiles) for h in range(2)], axis=0)


def _unstack_heads(stacked, first, tiles):
    parts = [jnp.where(first, stacked[(2 * t) * ATT_TILE:(2 * t + 1) * ATT_TILE],
                       stacked[(2 * t + 1) * ATT_TILE:(2 * t + 2) * ATT_TILE])
             for t in range(tiles)]
    return parts[0] if tiles == 1 else jnp.concatenate(parts, axis=0)


TILE_ROWS = 2 * ATT_TILE
STACK_ROWS = 2 * STEP_ROWS


def _stacked_diag_mask(n_rows, strict, later_tiles_open):
    row = lax.broadcasted_iota(jnp.int32, (n_rows, ATT_TILE), 0)
    col = lax.broadcasted_iota(jnp.int32, (n_rows, ATT_TILE), 1)
    pos = row & (ATT_TILE - 1)
    causal = (col < pos) if strict else (col <= pos)
    return jnp.logical_or(causal, row >= TILE_ROWS) if later_tiles_open else causal


def _sb_kernel(q_ref, k_ref, v_ref, tri_ref, o_ref, acc_ref, carry_ref):
    i = pl.program_id(2)
    first, second = _head_pair_masks()
    q = q_ref[0] * (HEAD_DIM ** -0.5)
    zero = jnp.zeros_like(q)
    q_stack = _stack_heads((jnp.where(first, q, zero), jnp.where(second, q, zero)), STEP_ROWS // ATT_TILE)
    tri = tri_ref[...]

    def front(row0, n_rows, j, mask):
        start = pl.multiple_of(j * ATT_TILE, ATT_TILE)
        k = k_ref[0, pl.ds(start, ATT_TILE), :]
        z = lax.dot_general(q_stack[row0:row0 + n_rows], k, _NT, preferred_element_type=F32)
        sp = _softplus(z)
        if mask is not None:
            sp = jnp.where(mask, sp, 0.0)
        hi = sp.astype(BF16)
        lo = (sp - hi.astype(F32)).astype(BF16)
        suffix = _dot(hi, tri) + _dot(lo, tri)
        return z - suffix, suffix[:, 0:1], v_ref[0, pl.ds(start, ATT_TILE), :]

    def back(exponent, carry, mask, v):
        w = jnp.exp(exponent - carry)
        if mask is not None:
            w = jnp.where(mask, w, 0.0)
        return _dot(w.astype(BF16), v)

    def visit(row0, n_rows, j):
        rows = pl.ds(row0, n_rows)
        exponent, total, v = front(row0, n_rows, j, None)
        acc_ref[rows, :] += back(exponent, carry_ref[rows, :], None, v)
        carry_ref[rows, :] += total

    first_block = (STEP_ROWS // ATT_TILE) * i
    has_previous = (i > 0).astype(F32)
    mask_late = _stacked_diag_mask(TILE_ROWS, True, False)
    mask_all = _stacked_diag_mask(STACK_ROWS, True, True)
    exp_late, total_late, v_late = front(TILE_ROWS, TILE_ROWS, first_block + 1, mask_late)
    exp_all, total_all, v_all = front(0, STACK_ROWS, first_block, mask_all)
    exp_prev, total_prev, v_prev = front(0, TILE_ROWS, jnp.maximum(first_block - 1, 0), None)
    no_carry = jnp.zeros((TILE_ROWS, 1), F32)
    pv_late = back(exp_late, no_carry, mask_late, v_late)
    pv_all = back(exp_all, jnp.concatenate([no_carry, total_late], axis=0), mask_all, v_all)
    pv_prev = back(exp_prev, total_all[:TILE_ROWS], None, v_prev) * has_previous
    acc_ref[...] = pv_all + jnp.concatenate([pv_prev, pv_late], axis=0)
    carry_ref[...] = total_all + jnp.concatenate([total_prev * has_previous, total_late], axis=0)

    def visit_earlier(tile):
        row0 = tile * TILE_ROWS

        def exhausted():
            return jnp.min(carry_ref[pl.ds(row0, TILE_ROWS), :]) > SKIP_BOUND

        def cond(state):
            j, done = state
            return jnp.logical_and(j >= 0, jnp.logical_not(done))

        def body(state):
            j, _ = state
            visit(row0, TILE_ROWS, j)
            return j - 1, exhausted()

        lax.while_loop(cond, body, (first_block + tile - 2, exhausted()))

    for tile in range(STEP_ROWS // ATT_TILE):
        visit_earlier(tile)
    o_ref[0] = _unstack_heads(acc_ref[...], first, STEP_ROWS // ATT_TILE).astype(o_ref.dtype)


def _sb_attention(qkv, batch, seq):
    return pl.pallas_call(
        _sb_kernel,
        grid=(batch, HEAD_PAIRS, seq // STEP_ROWS),
        in_specs=[
            pl.BlockSpec((1, STEP_ROWS, LANES), lambda b, p, i: (b, i, p)),
            pl.BlockSpec((1, seq, LANES), lambda b, p, i: (b, 0, HEAD_PAIRS + p)),
            pl.BlockSpec((1, seq, LANES), lambda b, p, i: (b, 0, 2 * HEAD_PAIRS + p)),
            pl.BlockSpec((ATT_TILE, ATT_TILE), lambda b, p, i: (0, 0)),
        ],
        out_specs=pl.BlockSpec((1, STEP_ROWS, LANES), lambda b, p, i: (b, i, p)),
        out_shape=jax.ShapeDtypeStruct((batch, seq, D_MODEL), BF16),
        scratch_shapes=[pltpu.VMEM((STACK_ROWS, LANES), F32), pltpu.VMEM((STACK_ROWS, 1), F32)],
        compiler_params=_params("parallel", "parallel", "arbitrary"),
    )(qkv, qkv, qkv, _suffix_ones())


def _split3(x):
    def top(v):
        return pltpu.bitcast(pltpu.bitcast(v, jnp.uint32) & jnp.uint32(0xFFFF0000), F32)
    a = top(x)
    r = x - a
    b = top(r)
    return a.astype(BF16), b.astype(BF16), (r - b).astype(BF16)


AUG_LANES = 6


def _fox_selectors():
    sel_q = np.zeros((HEAD_PAIRS, 3 * LANES, LANES), np.float32)
    sel_k = np.zeros((HEAD_PAIRS, 3 * LANES, LANES), np.float32)
    const_q = np.zeros((1, LANES), np.float32)
    const_k = np.zeros((1, LANES), np.float32)
    for m in range(2):
        for part in range(3):
            const_q[0, AUG_LANES * m + 3 + part] = 1.0
            const_k[0, AUG_LANES * m + part] = 1.0
            for p in range(HEAD_PAIRS):
                head = 2 * p + m
                sel_q[p, part * LANES + head, AUG_LANES * m + part] = 1.0
                sel_k[p, part * LANES + head, AUG_LANES * m + 3 + part] = -1.0
    return (jnp.asarray(sel_q, BF16), jnp.asarray(sel_k, BF16), jnp.asarray(const_q), jnp.asarray(const_k))


def _fox_cum_kernel(f_ref, b_ref, sq_ref, sk_ref, cq_ref, ck_ref, aq_ref, ak_ref, last_ref, carry_ref):
    @pl.when(pl.program_id(1) == 0)
    def _():
        carry_ref[...] = jnp.zeros_like(carry_ref)

    log_f = -_softplus(-(f_ref[0] + b_ref[...]))
    row = lax.broadcasted_iota(jnp.int32, (ATT_TILE, ATT_TILE), 0)
    col = lax.broadcasted_iota(jnp.int32, (ATT_TILE, ATT_TILE), 1)
    prefix_ones = (row >= col).astype(BF16)
    parts = _split3(log_f)
    cum = carry_ref[...] + _dot(prefix_ones, parts[0]) + _dot(prefix_ones, parts[1]) + _dot(prefix_ones, parts[2])
    carry_ref[...] = cum[ATT_TILE - 1:ATT_TILE, :]
    last_ref[0, 0] = cum[ATT_TILE - 1:ATT_TILE, :]
    stacked = jnp.concatenate(_split3(cum), axis=-1)
    for p in range(HEAD_PAIRS):
        aq_ref[0, p] = (_dot(stacked, sq_ref[p]) + cq_ref[...]).astype(BF16)
        ak_ref[0, p] = (_dot(stacked, sk_ref[p]) + ck_ref[...]).astype(BF16)


def _fox_cum(f_logits, bias, batch, seq):
    n_t = seq // ATT_TILE
    sel_q, sel_k, const_q, const_k = _fox_selectors()
    bias_row = jnp.zeros((1, LANES), F32).at[0, :ATT_HEADS].set(bias.astype(F32))
    whole = lambda shape: pl.BlockSpec(shape, lambda b, r: (0,) * len(shape))
    return pl.pallas_call(
        _fox_cum_kernel,
        grid=(batch, n_t),
        in_specs=[
            pl.BlockSpec((1, ATT_TILE, LANES), lambda b, r: (b, r, 0)),
            whole((1, LANES)),
            whole(sel_q.shape), whole(sel_k.shape), whole((1, LANES)), whole((1, LANES)),
        ],
        out_specs=[
            pl.BlockSpec((1, HEAD_PAIRS, ATT_TILE, LANES), lambda b, r: (b, 0, r, 0)),
            pl.BlockSpec((1, HEAD_PAIRS, ATT_TILE, LANES), lambda b, r: (b, 0, r, 0)),
            pl.BlockSpec((1, 1, 1, LANES), lambda b, r: (b, r, 0, 0)),
        ],
        out_shape=[
            jax.ShapeDtypeStruct((batch, HEAD_PAIRS, seq, LANES), BF16),
            jax.ShapeDtypeStruct((batch, HEAD_PAIRS, seq, LANES), BF16),
            jax.ShapeDtypeStruct((batch, n_t, 1, LANES), F32),
        ],
        scratch_shapes=[pltpu.VMEM((1, LANES), F32)],
        compiler_params=_params("parallel", "arbitrary"),
    )(f_logits, bias_row, sel_q, sel_k, const_q, const_k)


KNORM_ROWS = 512
FOX_FIRST_CHAINS = 2
FOX_CHAINS = 3


def _fox_kernel(last_ref, q_ref, aq_ref, k_ref, ak_ref, v_ref, o_ref, acc_ref, m_ref, l_ref, knorm_ref,
                *, n_blocks):
    b, p, i = pl.program_id(0), pl.program_id(1), pl.program_id(2)
    first, second = _head_pair_masks()
    head_lanes = (first, second)
    seq = k_ref.shape[1]

    @pl.when(i == 0)
    def _():
        def scan(c, best):
            kk = k_ref[0, pl.ds(pl.multiple_of(c * KNORM_ROWS, KNORM_ROWS), KNORM_ROWS), :].astype(F32)
            sq = kk * kk
            return tuple(jnp.maximum(best[h], jnp.sum(jnp.where(head_lanes[h], sq, 0.0), axis=-1, keepdims=True))
                         for h in range(2))
        init = (jnp.zeros((KNORM_ROWS, 1), F32),) * 2
        best = lax.fori_loop(0, seq // KNORM_ROWS, scan, init)
        for h in range(2):
            knorm_ref[h] = jnp.max(jnp.sqrt(best[h]))

    q = q_ref[0] * (HEAD_DIM ** -0.5)
    aq = aq_ref[0, 0]
    zero = jnp.zeros_like(q)
    lane = lax.broadcasted_iota(jnp.int32, (1, LANES), 1)
    aqf = aq.astype(F32)
    qf = q.astype(F32)
    q_heads, cum_t, q_norm = [], [], []
    for h in range(2):
        aug_lanes = jnp.logical_and(lane >= AUG_LANES * h, lane < AUG_LANES * (h + 1))
        q_heads.append(jnp.concatenate([jnp.where(head_lanes[h], q, zero), jnp.where(aug_lanes, aq, zero)], axis=-1))
        o = AUG_LANES * h
        cum_t.append(aqf[:, o:o + 1] + aqf[:, o + 1:o + 2] + aqf[:, o + 2:o + 3])
        q_norm.append(jnp.sqrt(jnp.sum(jnp.where(head_lanes[h], qf * qf, 0.0), axis=-1, keepdims=True)))
    row = lax.broadcasted_iota(jnp.int32, (ATT_TILE, ATT_TILE), 0)
    col = lax.broadcasted_iota(jnp.int32, (ATT_TILE, ATT_TILE), 1)
    causal = col <= row

    def local_softmax(h, j, mask, present):
        start = pl.multiple_of(jnp.maximum(j, 0) * ATT_TILE, ATT_TILE)
        kk = jnp.concatenate([k_ref[0, pl.ds(start, ATT_TILE), :], ak_ref[0, 0, pl.ds(start, ATT_TILE), :]], axis=-1)
        v = v_ref[0, pl.ds(start, ATT_TILE), :]
        s = lax.dot_general(q_heads[h], kk, _NT, preferred_element_type=F32)
        if mask is not None:
            s = jnp.where(mask, s, -1e30)
        if present is not None:
            s = jnp.where(present, s, -1e30)
        m = jnp.broadcast_to(jnp.max(s, axis=-1, keepdims=True), (ATT_TILE, LANES))
        prob = jnp.exp(s - jnp.concatenate([m, m], axis=-1))
        total = jnp.broadcast_to(jnp.sum(prob, axis=-1, keepdims=True), (ATT_TILE, LANES))
        return m, total, _dot(prob.astype(BF16), v)

    def merge(h, state, parts):
        m_old, l_old, acc_old = state
        m_new = functools.reduce(jnp.maximum, [part[0] for part in parts], m_old)
        scale = jnp.exp(m_old - m_new)
        l_new, acc_new = scale * l_old, scale * acc_old
        for m, total, pv in parts:
            scale = jnp.exp(m - m_new)
            l_new, acc_new = l_new + scale * total, acc_new + scale * pv
        m_ref[h], l_ref[h], acc_ref[h] = m_new, l_new, acc_new

    empty = (jnp.full((ATT_TILE, LANES), -1e30, F32), jnp.zeros((ATT_TILE, LANES), F32),
             jnp.zeros((ATT_TILE, LANES), F32))
    for h in range(2):
        merge(h, empty, [local_softmax(h, i, causal, None)]
              + [local_softmax(h, i - c, None, i >= c) for c in range(1, FOX_FIRST_CHAINS)])

    bound = [jnp.max(q_norm[h] * knorm_ref[h] + cum_t[h] - m_ref[h]) for h in range(2)]
    base = [(b * ATT_HEADS + 2 * p + h) * n_blocks for h in range(2)]

    def cond(j):
        jj = jnp.maximum(j, 0)
        live = jnp.logical_or(bound[0] - last_ref[base[0] + jj] >= -SKIP_BOUND,
                              bound[1] - last_ref[base[1] + jj] >= -SKIP_BOUND)
        return jnp.logical_and(j >= 0, live)

    def body(j):
        parts = [[local_softmax(h, j, None, None)]
                 + [local_softmax(h, j - c, None, j >= c) for c in range(1, FOX_CHAINS)] for h in range(2)]
        states = [(m_ref[h], l_ref[h], acc_ref[h]) for h in range(2)]
        for h in range(2):
            merge(h, states[h], parts[h])
        return j - FOX_CHAINS

    lax.while_loop(cond, body, i - FOX_FIRST_CHAINS)
    o_ref[0] = jnp.where(first, acc_ref[0] / l_ref[0], acc_ref[1] / l_ref[1]).astype(o_ref.dtype)


def _fox_attention(qkv, aug_q, aug_k, cum_last, batch, seq):
    grid_spec = pltpu.PrefetchScalarGridSpec(
        num_scalar_prefetch=1,
        grid=(batch, HEAD_PAIRS, seq // ATT_TILE),
        in_specs=[
            pl.BlockSpec((1, ATT_TILE, LANES), lambda b, p, i, t: (b, i, p)),
            pl.BlockSpec((1, 1, ATT_TILE, LANES), lambda b, p, i, t: (b, p, i, 0)),
            pl.BlockSpec((1, seq, LANES), lambda b, p, i, t: (b, 0, HEAD_PAIRS + p)),
            pl.BlockSpec((1, 1, seq, LANES), lambda b, p, i, t: (b, p, 0, 0)),
            pl.BlockSpec((1, seq, LANES), lambda b, p, i, t: (b, 0, 2 * HEAD_PAIRS + p)),
        ],
        out_specs=pl.BlockSpec((1, ATT_TILE, LANES), lambda b, p, i, t: (b, i, p)),
        scratch_shapes=[
            pltpu.VMEM((2, ATT_TILE, LANES), F32),
            pltpu.VMEM((2, ATT_TILE, LANES), F32),
            pltpu.VMEM((2, ATT_TILE, LANES), F32),
            pltpu.SMEM((2,), F32),
        ],
    )
    return pl.pallas_call(
        functools.partial(_fox_kernel, n_blocks=seq // ATT_TILE),
        grid_spec=grid_spec,
        out_shape=jax.ShapeDtypeStruct((batch, seq, D_MODEL), BF16),
        compiler_params=_params("parallel", "parallel", "arbitrary"),
    )(cum_last, qkv, aug_q, qkv, aug_k, qkv)


def _retention_tables(seq):
    half = RET_QK // 2
    inv_freq = ROPE_BASE ** (-2.0 * jnp.arange(half, dtype=F32) / RET_QK)
    ang = jnp.arange(seq)[:, None].astype(F32) * inv_freq[None, :]
    log_gamma = jnp.log1p(-jnp.exp2(-5.0 - jnp.arange(RET_HEADS, dtype=F32)))
    idx = jnp.arange(RET_CHUNK, dtype=F32)
    diff = idx[:, None] - idx[None, :]
    intra = jnp.where(diff >= 0, jnp.exp(log_gamma[:, None, None] * jnp.maximum(diff, 0.0)), 0.0)
    q_decay = jnp.exp(log_gamma[:, None] * (idx + 1.0))[:, :, None]
    k_decay = jnp.exp(log_gamma[:, None] * (RET_CHUNK - 1.0 - idx))[:, :, None]
    chunk_decay = jnp.broadcast_to(jnp.exp(log_gamma * RET_CHUNK)[:, None, None], (RET_HEADS, 1, RET_V))
    return jnp.cos(ang), jnp.sin(ang), intra, q_decay, k_decay, chunk_decay


def _rotary(x, cos, sin):
    half = RET_QK // 2
    x1, x2 = x[:, :half], x[:, half:]
    return jnp.concatenate([x1 * cos - x2 * sin, x1 * sin + x2 * cos], axis=-1)


def _retention_kernel(q_ref, k_ref, v_ref, g_ref, cos_ref, sin_ref, intra_ref, qd_ref, kd_ref, cd_ref, gain_ref,
                      o_ref, state_ref):
    @pl.when(pl.program_id(2) == 0)
    def _():
        state_ref[...] = jnp.zeros_like(state_ref)

    intra, q_decay, k_decay, chunk_decay = intra_ref[0], qd_ref[0], kd_ref[0], cd_ref[0]
    for c in range(RET_ROWS // RET_CHUNK):
        rows = pl.ds(c * RET_CHUNK, RET_CHUNK)
        cos, sin = cos_ref[rows, :], sin_ref[rows, :]
        qc = _rotary(q_ref[0, rows, :].astype(F32), cos, sin) * (RET_QK ** -0.5)
        kc = _rotary(k_ref[0, rows, :].astype(F32), cos, sin)
        vc = v_ref[0, rows, :]
        state = state_ref[...]
        scores = lax.dot_general(qc.astype(BF16), kc.astype(BF16), _NT, preferred_element_type=F32) * intra
        y = _dot(scores.astype(BF16), vc) + _dot((qc * q_decay).astype(BF16), state.astype(BF16))
        state_ref[...] = state * chunk_decay + lax.dot_general(
            (kc * k_decay).astype(BF16), vc, _TN, preferred_element_type=F32)
        mu = jnp.mean(y, axis=-1, keepdims=True)
        yc = y - mu
        var = jnp.mean(yc * yc, axis=-1, keepdims=True)
        yn = yc * lax.rsqrt(var + GN_EPS) * gain_ref[...]
        g = g_ref[0, rows, :].astype(F32)
        o_ref[0, rows, :] = (g / (1.0 + jnp.exp(-g)) * yn).astype(o_ref.dtype)


def _retention(proj, gn_gain, batch, seq):
    cos, sin, intra, q_decay, k_decay, chunk_decay = _retention_tables(seq)
    qk_blocks = D_MODEL // RET_QK
    v_blocks = 2 * D_MODEL // RET_V
    per_head = lambda shape: pl.BlockSpec((1,) + shape, lambda b, h, r: (h, 0, 0))
    return pl.pallas_call(
        _retention_kernel,
        grid=(batch, RET_HEADS, seq // RET_ROWS),
        in_specs=[
            pl.BlockSpec((1, RET_ROWS, RET_QK), lambda b, h, r: (b, r, h)),
            pl.BlockSpec((1, RET_ROWS, RET_QK), lambda b, h, r: (b, r, qk_blocks + h)),
            pl.BlockSpec((1, RET_ROWS, RET_V), lambda b, h, r: (b, r, v_blocks + h)),
            pl.BlockSpec((1, RET_ROWS, RET_V), lambda b, h, r: (b, r, v_blocks + RET_HEADS + h)),
            pl.BlockSpec((RET_ROWS, RET_QK // 2), lambda b, h, r: (r, 0)),
            pl.BlockSpec((RET_ROWS, RET_QK // 2), lambda b, h, r: (r, 0)),
            per_head((RET_CHUNK, RET_CHUNK)),
            per_head((RET_CHUNK, 1)),
            per_head((RET_CHUNK, 1)),
            per_head((1, RET_V)),
            pl.BlockSpec((1, RET_V), lambda b, h, r: (0, h)),
        ],
        out_specs=pl.BlockSpec((1, RET_ROWS, RET_V), lambda b, h, r: (b, r, h)),
        out_shape=jax.ShapeDtypeStruct((batch, seq, RET_HEADS * RET_V), BF16),
        scratch_shapes=[pltpu.VMEM((RET_QK, RET_V), F32)],
        compiler_params=_params("parallel", "parallel", "arbitrary"),
    )(proj, proj, proj, proj, cos, sin, intra, q_decay, k_decay, chunk_decay,
      gn_gain.astype(F32).reshape(1, RET_HEADS * RET_V))


def _sb_mixer(x, norm, w_in, batch, seq):
    qkv = _norm_matmul(x, norm, w_in.astype(BF16), BF16, 1024)
    return _sb_attention(qkv.reshape(batch, seq, -1), batch, seq).reshape(batch * seq, -1)


def _retention_mixer(x, norm, w_in, gn_gain, batch, seq):
    proj = _norm_matmul(x, norm, w_in.astype(BF16), BF16, 1024)
    return _retention(proj.reshape(batch, seq, -1), gn_gain, batch, seq).reshape(batch * seq, -1)


def _fox_mixer(x, norm, w_in, b_forget, batch, seq):
    w_qkv = w_in[:, :3 * D_MODEL].astype(BF16)
    w_f = jnp.pad(w_in[:, 3 * D_MODEL:], ((0, 0), (0, LANES - ATT_HEADS))).astype(BF16)
    qkv, f_logits = _norm_matmul_gate(x, norm, w_qkv, w_f, 1024)
    aug_q, aug_k, cum_last = _fox_cum(f_logits.reshape(batch, seq, LANES), b_forget, batch, seq)
    last_table = cum_last[:, :, 0, :ATT_HEADS].transpose(0, 2, 1).reshape(-1)
    o = _fox_attention(qkv.reshape(batch, seq, -1), aug_q, aug_k, last_table, batch, seq)
    return o.reshape(batch * seq, -1)


def kernel(x, norm_mix_0, w_in_0, w_out_0, norm_mlp_0, w_up_0, w_down_0, norm_mix_1, w_in_1, gn_gain_1, w_out_1, norm_mlp_1, w_up_1, w_down_1, norm_mix_2, w_in_2, b_forget_2, w_out_2, norm_mlp_2, w_up_2, w_down_2, norm_mix_3, w_in_3, w_out_3, norm_mlp_3, w_up_3, w_down_3, norm_final):
    batch, seq, d = x.shape
    h = x.reshape(batch * seq, d)
    tails = ((w_out_0, norm_mlp_0, w_up_0, w_down_0), (w_out_1, norm_mlp_1, w_up_1, w_down_1),
             (w_out_2, norm_mlp_2, w_up_2, w_down_2), (w_out_3, norm_mlp_3, w_up_3, w_down_3))
    for layer in range(4):
        if layer == 0:
            a = _sb_mixer(h, norm_mix_0, w_in_0, batch, seq)
        elif layer == 1:
            a = _retention_mixer(h, norm_mix_1, w_in_1, gn_gain_1, batch, seq)
        elif layer == 2:
            a = _fox_mixer(h, norm_mix_2, w_in_2, b_forget_2, batch, seq)
        else:
            a = _sb_mixer(h, norm_mix_3, w_in_3, batch, seq)
        w_out, norm_mlp, w_up, w_down = tails[layer]
        h = _mixer_out_mlp(a, w_out.astype(BF16), h, norm_mlp, w_up.astype(BF16), w_down.astype(BF16),
                           norm_final, layer == 3)
    return h.reshape(batch, seq, d)
```

```python
import functools

import jax
import jax.numpy as jnp
import numpy as np
from jax import lax
from jax.experimental import pallas as pl
from jax.experimental.pallas import tpu as pltpu

F32 = jnp.float32
BF16 = jnp.bfloat16

D_MODEL = 1024
D_FF = 4 * D_MODEL
HEAD_DIM = 64
ATT_HEADS = D_MODEL // HEAD_DIM
LANES = 128
HEAD_PAIRS = D_MODEL // LANES
RET_HEADS = 4
RET_QK = D_MODEL // RET_HEADS
RET_V = 2 * RET_QK
RET_CHUNK = 128
ROPE_BASE = 10000.0
NORM_EPS = 1e-6
GN_EPS = 1e-5

ROW_TILE = 1024
ATT_TILE = 256
STEP_ROWS = 2 * ATT_TILE
RET_ROWS = 512
SKIP_BOUND = 105.0
VMEM_LIMIT = 56 * 1024 * 1024

_NT = (((1,), (1,)), ((), ()))
_TN = (((0,), (0,)), ((), ()))


def _params(*semantics):
    return pltpu.CompilerParams(dimension_semantics=semantics, vmem_limit_bytes=VMEM_LIMIT)


def _dot(a, b):
    return jnp.dot(a, b, preferred_element_type=F32)


def _rms_normed(x, gain):
    return x * lax.rsqrt(jnp.mean(x * x, axis=-1, keepdims=True) + NORM_EPS) * gain


def _softplus(x):
    return jnp.maximum(x, 0.0) + jnp.log(1.0 + jnp.exp(-jnp.abs(x)))


def _norm_matmul_kernel(x_ref, g_ref, w_ref, o_ref, h_ref):
    @pl.when(pl.program_id(1) == 0)
    def _():
        h_ref[...] = _rms_normed(x_ref[...], g_ref[...]).astype(BF16)

    o_ref[...] = _dot(h_ref[...], w_ref[...]).astype(o_ref.dtype)


def _norm_matmul(x, gain, w, out_dtype, col_tile):
    t, d = x.shape
    n = w.shape[1]
    return pl.pallas_call(
        _norm_matmul_kernel,
        grid=(t // ROW_TILE, n // col_tile),
        in_specs=[
            pl.BlockSpec((ROW_TILE, d), lambda i, j: (i, 0)),
            pl.BlockSpec((1, d), lambda i, j: (0, 0)),
            pl.BlockSpec((d, col_tile), lambda i, j: (0, j)),
        ],
        out_specs=pl.BlockSpec((ROW_TILE, col_tile), lambda i, j: (i, j)),
        out_shape=jax.ShapeDtypeStruct((t, n), out_dtype),
        scratch_shapes=[pltpu.VMEM((ROW_TILE, d), BF16)],
        compiler_params=_params("parallel", "arbitrary"),
    )(x, gain.reshape(1, d), w)


def _norm_matmul_gate_kernel(x_ref, g_ref, w_ref, wf_ref, o_ref, f_ref, h_ref):
    @pl.when(pl.program_id(1) == 0)
    def _():
        h_ref[...] = _rms_normed(x_ref[...], g_ref[...]).astype(BF16)
        f_ref[...] = _dot(h_ref[...], wf_ref[...])

    o_ref[...] = _dot(h_ref[...], w_ref[...]).astype(o_ref.dtype)


def _norm_matmul_gate(x, gain, w, w_gate, col_tile):
    t, d = x.shape
    n = w.shape[1]
    return pl.pallas_call(
        _norm_matmul_gate_kernel,
        grid=(t // ROW_TILE, n // col_tile),
        in_specs=[
            pl.BlockSpec((ROW_TILE, d), lambda i, j: (i, 0)),
            pl.BlockSpec((1, d), lambda i, j: (0, 0)),
            pl.BlockSpec((d, col_tile), lambda i, j: (0, j)),
            pl.BlockSpec((d, LANES), lambda i, j: (0, 0)),
        ],
        out_specs=[pl.BlockSpec((ROW_TILE, col_tile), lambda i, j: (i, j)),
                   pl.BlockSpec((ROW_TILE, LANES), lambda i, j: (i, 0))],
        out_shape=[jax.ShapeDtypeStruct((t, n), BF16), jax.ShapeDtypeStruct((t, LANES), F32)],
        scratch_shapes=[pltpu.VMEM((ROW_TILE, d), BF16)],
        compiler_params=_params("parallel", "arbitrary"),
    )(x, gain.reshape(1, d), w, w_gate)


def _mixer_out_mlp_kernel(a_ref, wo_ref, x_ref, g_ref, wu_ref, wd_ref, gf_ref, o_ref, h_ref, *, final_norm):
    f = pl.program_id(1)

    @pl.when(f == 0)
    def _():
        o_ref[...] = x_ref[...] + _dot(a_ref[...], wo_ref[...])
        h_ref[...] = _rms_normed(o_ref[...], g_ref[...]).astype(BF16)

    u = jnp.maximum(_dot(h_ref[...], wu_ref[...]), 0.0)
    o_ref[...] += _dot((u * u).astype(BF16), wd_ref[...])

    if final_norm:
        @pl.when(f == pl.num_programs(1) - 1)
        def _():
            o_ref[...] = _rms_normed(o_ref[...], gf_ref[...])


def _mixer_out_mlp(a, w_out, x, gain, w_up, w_down, final_gain, final_norm):
    t, d = x.shape
    k = a.shape[1]
    ff = w_up.shape[1]
    ff_tile = 1024
    return pl.pallas_call(
        functools.partial(_mixer_out_mlp_kernel, final_norm=final_norm),
        grid=(t // ROW_TILE, ff // ff_tile),
        in_specs=[
            pl.BlockSpec((ROW_TILE, k), lambda i, f: (i, 0)),
            pl.BlockSpec((k, d), lambda i, f: (0, 0)),
            pl.BlockSpec((ROW_TILE, d), lambda i, f: (i, 0)),
            pl.BlockSpec((1, d), lambda i, f: (0, 0)),
            pl.BlockSpec((d, ff_tile), lambda i, f: (0, f)),
            pl.BlockSpec((ff_tile, d), lambda i, f: (f, 0)),
            pl.BlockSpec((1, d), lambda i, f: (0, 0)),
        ],
        out_specs=pl.BlockSpec((ROW_TILE, d), lambda i, f: (i, 0)),
        out_shape=jax.ShapeDtypeStruct((t, d), F32),
        scratch_shapes=[pltpu.VMEM((ROW_TILE, d), BF16)],
        compiler_params=_params("parallel", "arbitrary"),
    )(a, w_out, x, gain.reshape(1, d), w_up, w_down, final_gain.reshape(1, d))


def _head_pair_masks():
    lane = lax.broadcasted_iota(jnp.int32, (1, LANES), 1)
    return lane < HEAD_DIM, lane >= HEAD_DIM


def _suffix_ones():
    idx = np.arange(ATT_TILE)
    return jnp.asarray(idx[:, None] >= idx[None, :], BF16)


def _stack_heads(x_heads, tiles):
    return jnp.concatenate([x_heads[h][t * ATT_TILE:(t + 1) * ATT_TILE]
                            for t in range(tiles) for h in range(2)], axis=0)


def _unstack_heads(stacked, first, tiles):
    parts = [jnp.where(first, stacked[(2 * t) * ATT_TILE:(2 * t + 1) * ATT_TILE],
                       stacked[(2 * t + 1) * ATT_TILE:(2 * t + 2) * ATT_TILE])
             for t in range(tiles)]
    return parts[0] if tiles == 1 else jnp.concatenate(parts, axis=0)


TILE_ROWS = 2 * ATT_TILE
STACK_ROWS = 2 * STEP_ROWS


def _stacked_diag_mask(n_rows, strict, later_tiles_open):
    row = lax.broadcasted_iota(jnp.int32, (n_rows, ATT_TILE), 0)
    col = lax.broadcasted_iota(jnp.int32, (n_rows, ATT_TILE), 1)
    pos = row & (ATT_TILE - 1)
    causal = (col < pos) if strict else (col <= pos)
    return jnp.logical_or(causal, row >= TILE_ROWS) if later_tiles_open else causal


def _sb_kernel(q_ref, k_ref, v_ref, tri_ref, o_ref, acc_ref, carry_ref):
    i = pl.program_id(2)
    first, second = _head_pair_masks()
    q = q_ref[0] * (HEAD_DIM ** -0.5)
    zero = jnp.zeros_like(q)
    q_stack = _stack_heads((jnp.where(first, q, zero), jnp.where(second, q, zero)), STEP_ROWS // ATT_TILE)
    tri = tri_ref[...]

    def front(row0, n_rows, j, mask):
        start = pl.multiple_of(j * ATT_TILE, ATT_TILE)
        k = k_ref[0, pl.ds(start, ATT_TILE), :]
        z = lax.dot_general(q_stack[row0:row0 + n_rows], k, _NT, preferred_element_type=F32)
        sp = _softplus(z)
        if mask is not None:
            sp = jnp.where(mask, sp, 0.0)
        hi = sp.astype(BF16)
        lo = (sp - hi.astype(F32)).astype(BF16)
        suffix = _dot(hi, tri) + _dot(lo, tri)
        return z - suffix, suffix[:, 0:1], v_ref[0, pl.ds(start, ATT_TILE), :]

    def back(exponent, carry, mask, v):
        w = jnp.exp(exponent - carry)
        if mask is not None:
            w = jnp.where(mask, w, 0.0)
        return _dot(w.astype(BF16), v)

    def visit(row0, n_rows, j):
        rows = pl.ds(row0, n_rows)
        exponent, total, v = front(row0, n_rows, j, None)
        acc_ref[rows, :] += back(exponent, carry_ref[rows, :], None, v)
        carry_ref[rows, :] += total

    first_block = (STEP_ROWS // ATT_TILE) * i
    has_previous = (i > 0).astype(F32)
    mask_late = _stacked_diag_mask(TILE_ROWS, True, False)
    mask_all = _stacked_diag_mask(STACK_ROWS, True, True)
    exp_late, total_late, v_late = front(TILE_ROWS, TILE_ROWS, first_block + 1, mask_late)
    exp_all, total_all, v_all = front(0, STACK_ROWS, first_block, mask_all)
    exp_prev, total_prev, v_prev = front(0, TILE_ROWS, jnp.maximum(first_block - 1, 0), None)
    no_carry = jnp.zeros((TILE_ROWS, 1), F32)
    pv_late = back(exp_late, no_carry, mask_late, v_late)
    pv_all = back(exp_all, jnp.concatenate([no_carry, total_late], axis=0), mask_all, v_all)
    pv_prev = back(exp_prev, total_all[:TILE_ROWS], None, v_prev) * has_previous
    acc_ref[...] = pv_all + jnp.concatenate([pv_prev, pv_late], axis=0)
    carry_ref[...] = total_all + jnp.concatenate([total_prev * has_previous, total_late], axis=0)

    def visit_earlier(tile):
        row0 = tile * TILE_ROWS

        def exhausted():
            return jnp.min(carry_ref[pl.ds(row0, TILE_ROWS), :]) > SKIP_BOUND

        def cond(state):
            j, done = state
            return jnp.logical_and(j >= 0, jnp.logical_not(done))

        def body(state):
            j, _ = state
            visit(row0, TILE_ROWS, j)
            return j - 1, exhausted()

        lax.while_loop(cond, body, (first_block + tile - 2, exhausted()))

    for tile in range(STEP_ROWS // ATT_TILE):
        visit_earlier(tile)
    o_ref[0] = _unstack_heads(acc_ref[...], first, STEP_ROWS // ATT_TILE).astype(o_ref.dtype)


def _sb_attention(qkv, batch, seq):
    return pl.pallas_call(
        _sb_kernel,
        grid=(batch, HEAD_PAIRS, seq // STEP_ROWS),
        in_specs=[
            pl.BlockSpec((1, STEP_ROWS, LANES), lambda b, p, i: (b, i, p)),
            pl.BlockSpec((1, seq, LANES), lambda b, p, i: (b, 0, HEAD_PAIRS + p)),
            pl.BlockSpec((1, seq, LANES), lambda b, p, i: (b, 0, 2 * HEAD_PAIRS + p)),
            pl.BlockSpec((ATT_TILE, ATT_TILE), lambda b, p, i: (0, 0)),
        ],
        out_specs=pl.BlockSpec((1, STEP_ROWS, LANES), lambda b, p, i: (b, i, p)),
        out_shape=jax.ShapeDtypeStruct((batch, seq, D_MODEL), BF16),
        scratch_shapes=[pltpu.VMEM((STACK_ROWS, LANES), F32), pltpu.VMEM((STACK_ROWS, 1), F32)],
        compiler_params=_params("parallel", "parallel", "arbitrary"),
    )(qkv, qkv, qkv, _suffix_ones())


def _split3(x):
    def top(v):
        return pltpu.bitcast(pltpu.bitcast(v, jnp.uint32) & jnp.uint32(0xFFFF0000), F32)
    a = top(x)
    r = x - a
    b = top(r)
    return a.astype(BF16), b.astype(BF16), (r - b).astype(BF16)


AUG_LANES = 6


def _fox_selectors():
    sel_q = np.zeros((3 * LANES, HEAD_PAIRS * LANES), np.float32)
    sel_k = np.zeros((3 * LANES, HEAD_PAIRS * LANES), np.float32)
    const_q = np.zeros((1, LANES), np.float32)
    const_k = np.zeros((1, LANES), np.float32)
    for m in range(2):
        for part in range(3):
            const_q[0, AUG_LANES * m + 3 + part] = 1.0
            const_k[0, AUG_LANES * m + part] = 1.0
            for p in range(HEAD_PAIRS):
                head = 2 * p + m
                sel_q[part * LANES + head, p * LANES + AUG_LANES * m + part] = 1.0
                sel_k[part * LANES + head, p * LANES + AUG_LANES * m + 3 + part] = -1.0
    return (jnp.asarray(sel_q, BF16), jnp.asarray(sel_k, BF16), jnp.asarray(const_q), jnp.asarray(const_k))


def _fox_cum_kernel(f_ref, b_ref, sq_ref, sk_ref, cq_ref, ck_ref, aq_ref, ak_ref, last_ref, carry_ref):
    @pl.when(pl.program_id(1) == 0)
    def _():
        carry_ref[...] = jnp.zeros_like(carry_ref)

    log_f = -_softplus(-(f_ref[0] + b_ref[...]))
    row = lax.broadcasted_iota(jnp.int32, (ATT_TILE, ATT_TILE), 0)
    col = lax.broadcasted_iota(jnp.int32, (ATT_TILE, ATT_TILE), 1)
    prefix_ones = (row >= col).astype(BF16)
    parts = _split3(log_f)
    cum = carry_ref[...] + _dot(prefix_ones, parts[0]) + _dot(prefix_ones, parts[1]) + _dot(prefix_ones, parts[2])
    carry_ref[...] = cum[ATT_TILE - 1:ATT_TILE, :]
    last_ref[0, 0] = cum[ATT_TILE - 1:ATT_TILE, :]
    stacked = jnp.concatenate(_split3(cum), axis=-1)
    wide_q = _dot(stacked, sq_ref[...])
    wide_k = _dot(stacked, sk_ref[...])
    for p in range(HEAD_PAIRS):
        lanes = slice(p * LANES, (p + 1) * LANES)
        aq_ref[0, p] = (wide_q[:, lanes] + cq_ref[...]).astype(BF16)
        ak_ref[0, p] = (wide_k[:, lanes] + ck_ref[...]).astype(BF16)


def _fox_cum(f_logits, bias, batch, seq):
    n_t = seq // ATT_TILE
    sel_q, sel_k, const_q, const_k = _fox_selectors()
    bias_row = jnp.zeros((1, LANES), F32).at[0, :ATT_HEADS].set(bias.astype(F32))
    whole = lambda shape: pl.BlockSpec(shape, lambda b, r: (0,) * len(shape))
    return pl.pallas_call(
        _fox_cum_kernel,
        grid=(batch, n_t),
        in_specs=[
            pl.BlockSpec((1, ATT_TILE, LANES), lambda b, r: (b, r, 0)),
            whole((1, LANES)),
            whole(sel_q.shape), whole(sel_k.shape), whole((1, LANES)), whole((1, LANES)),
        ],
        out_specs=[
            pl.BlockSpec((1, HEAD_PAIRS, ATT_TILE, LANES), lambda b, r: (b, 0, r, 0)),
            pl.BlockSpec((1, HEAD_PAIRS, ATT_TILE, LANES), lambda b, r: (b, 0, r, 0)),
            pl.BlockSpec((1, 1, 1, LANES), lambda b, r: (b, r, 0, 0)),
        ],
        out_shape=[
            jax.ShapeDtypeStruct((batch, HEAD_PAIRS, seq, LANES), BF16),
            jax.ShapeDtypeStruct((batch, HEAD_PAIRS, seq, LANES), BF16),
            jax.ShapeDtypeStruct((batch, n_t, 1, LANES), F32),
        ],
        scratch_shapes=[pltpu.VMEM((1, LANES), F32)],
        compiler_params=_params("parallel", "arbitrary"),
    )(f_logits, bias_row, sel_q, sel_k, const_q, const_k)


KNORM_ROWS = 512


def _fox_kernel(last_ref, q_ref, aq_ref, k_ref, ak_ref, v_ref, o_ref, acc_ref, m_ref, l_ref, knorm_ref,
                *, n_blocks):
    b, p, i = pl.program_id(0), pl.program_id(1), pl.program_id(2)
    first, second = _head_pair_masks()
    head_lanes = (first, second)
    seq = k_ref.shape[1]

    @pl.when(i == 0)
    def _():
        def scan(c, best):
            kk = k_ref[0, pl.ds(pl.multiple_of(c * KNORM_ROWS, KNORM_ROWS), KNORM_ROWS), :].astype(F32)
            sq = kk * kk
            return tuple(jnp.maximum(best[h], jnp.sum(jnp.where(head_lanes[h], sq, 0.0), axis=-1, keepdims=True))
                         for h in range(2))
        init = (jnp.zeros((KNORM_ROWS, 1), F32),) * 2
        best = lax.fori_loop(0, seq // KNORM_ROWS, scan, init)
        for h in range(2):
            knorm_ref[h] = jnp.max(jnp.sqrt(best[h]))

    q = q_ref[0] * (HEAD_DIM ** -0.5)
    aq = aq_ref[0, 0]
    zero = jnp.zeros_like(q)
    lane = lax.broadcasted_iota(jnp.int32, (1, LANES), 1)
    aqf = aq.astype(F32)
    qf = q.astype(F32)
    q_heads, cum_t, q_norm = [], [], []
    for h in range(2):
        aug_lanes = jnp.logical_and(lane >= AUG_LANES * h, lane < AUG_LANES * (h + 1))
        q_heads.append(jnp.concatenate([jnp.where(head_lanes[h], q, zero), jnp.where(aug_lanes, aq, zero)], axis=-1))
        o = AUG_LANES * h
        cum_t.append(aqf[:, o:o + 1] + aqf[:, o + 1:o + 2] + aqf[:, o + 2:o + 3])
        q_norm.append(jnp.sqrt(jnp.sum(jnp.where(head_lanes[h], qf * qf, 0.0), axis=-1, keepdims=True)))
    row = lax.broadcasted_iota(jnp.int32, (ATT_TILE, ATT_TILE), 0)
    col = lax.broadcasted_iota(jnp.int32, (ATT_TILE, ATT_TILE), 1)
    causal = col <= row

    def local_softmax(h, j, mask, present):
        start = pl.multiple_of(jnp.maximum(j, 0) * ATT_TILE, ATT_TILE)
        kk = jnp.concatenate([k_ref[0, pl.ds(start, ATT_TILE), :], ak_ref[0, 0, pl.ds(start, ATT_TILE), :]], axis=-1)
        v = v_ref[0, pl.ds(start, ATT_TILE), :]
        s = lax.dot_general(q_heads[h], kk, _NT, preferred_element_type=F32)
        if mask is not None:
            s = jnp.where(mask, s, -1e30)
        if present is not None:
            s = jnp.where(present, s, -1e30)
        m = jnp.broadcast_to(jnp.max(s, axis=-1, keepdims=True), (ATT_TILE, LANES))
        prob = jnp.exp(s - jnp.concatenate([m, m], axis=-1))
        total = jnp.broadcast_to(jnp.sum(prob, axis=-1, keepdims=True), (ATT_TILE, LANES))
        return m, total, _dot(prob.astype(BF16), v)

    def merge(h, state, parts):
        m_old, l_old, acc_old = state
        m_new = functools.reduce(jnp.maximum, [part[0] for part in parts], m_old)
        scale = jnp.exp(m_old - m_new)
        l_new, acc_new = scale * l_old, scale * acc_old
        for m, total, pv in parts:
            scale = jnp.exp(m - m_new)
            l_new, acc_new = l_new + scale * total, acc_new + scale * pv
        m_ref[h], l_ref[h], acc_ref[h] = m_new, l_new, acc_new

    empty = (jnp.full((ATT_TILE, LANES), -1e30, F32), jnp.zeros((ATT_TILE, LANES), F32),
             jnp.zeros((ATT_TILE, LANES), F32))
    for h in range(2):
        merge(h, empty, [local_softmax(h, i, causal, None), local_softmax(h, i - 1, None, i > 0)])

    bound = [jnp.max(q_norm[h] * knorm_ref[h] + cum_t[h] - m_ref[h]) for h in range(2)]
    base = [(b * ATT_HEADS + 2 * p + h) * n_blocks for h in range(2)]

    def cond(j):
        jj = jnp.maximum(j, 0)
        live = jnp.logical_or(bound[0] - last_ref[base[0] + jj] >= -SKIP_BOUND,
                              bound[1] - last_ref[base[1] + jj] >= -SKIP_BOUND)
        return jnp.logical_and(j >= 0, live)

    def body(j):
        parts = [[local_softmax(h, j, None, None), local_softmax(h, j - 1, None, j > 0)] for h in range(2)]
        states = [(m_ref[h], l_ref[h], acc_ref[h]) for h in range(2)]
        for h in range(2):
            merge(h, states[h], parts[h])
        return j - 2

    lax.while_loop(cond, body, i - 2)
    o_ref[0] = jnp.where(first, acc_ref[0] / l_ref[0], acc_ref[1] / l_ref[1]).astype(o_ref.dtype)


def _fox_attention(qkv, aug_q, aug_k, cum_last, batch, seq):
    grid_spec = pltpu.PrefetchScalarGridSpec(
        num_scalar_prefetch=1,
        grid=(batch, HEAD_PAIRS, seq // ATT_TILE),
        in_specs=[
            pl.BlockSpec((1, ATT_TILE, LANES), lambda b, p, i, t: (b, i, p)),
            pl.BlockSpec((1, 1, ATT_TILE, LANES), lambda b, p, i, t: (b, p, i, 0)),
            pl.BlockSpec((1, seq, LANES), lambda b, p, i, t: (b, 0, HEAD_PAIRS + p)),
            pl.BlockSpec((1, 1, seq, LANES), lambda b, p, i, t: (b, p, 0, 0)),
            pl.BlockSpec((1, seq, LANES), lambda b, p, i, t: (b, 0, 2 * HEAD_PAIRS + p)),
        ],
        out_specs=pl.BlockSpec((1, ATT_TILE, LANES), lambda b, p, i, t: (b, i, p)),
        scratch_shapes=[
            pltpu.VMEM((2, ATT_TILE, LANES), F32),
            pltpu.VMEM((2, ATT_TILE, LANES), F32),
            pltpu.VMEM((2, ATT_TILE, LANES), F32),
            pltpu.SMEM((2,), F32),
        ],
    )
    return pl.pallas_call(
        functools.partial(_fox_kernel, n_blocks=seq // ATT_TILE),
        grid_spec=grid_spec,
        out_shape=jax.ShapeDtypeStruct((batch, seq, D_MODEL), BF16),
        compiler_params=_params("parallel", "parallel", "arbitrary"),
    )(cum_last, qkv, aug_q, qkv, aug_k, qkv)


def _retention_tables(seq):
    half = RET_QK // 2
    inv_freq = ROPE_BASE ** (-2.0 * jnp.arange(half, dtype=F32) / RET_QK)
    ang = jnp.arange(seq)[:, None].astype(F32) * inv_freq[None, :]
    log_gamma = jnp.log1p(-jnp.exp2(-5.0 - jnp.arange(RET_HEADS, dtype=F32)))
    idx = jnp.arange(RET_CHUNK, dtype=F32)
    diff = idx[:, None] - idx[None, :]
    intra = jnp.where(diff >= 0, jnp.exp(log_gamma[:, None, None] * jnp.maximum(diff, 0.0)), 0.0)
    q_decay = jnp.exp(log_gamma[:, None] * (idx + 1.0))[:, :, None]
    k_decay = jnp.exp(log_gamma[:, None] * (RET_CHUNK - 1.0 - idx))[:, :, None]
    chunk_decay = jnp.broadcast_to(jnp.exp(log_gamma * RET_CHUNK)[:, None, None], (RET_HEADS, 1, RET_V))
    return jnp.cos(ang), jnp.sin(ang), intra, q_decay, k_decay, chunk_decay


def _rotary(x, cos, sin):
    half = RET_QK // 2
    x1, x2 = x[:, :half], x[:, half:]
    return jnp.concatenate([x1 * cos - x2 * sin, x1 * sin + x2 * cos], axis=-1)


def _retention_kernel(q_ref, k_ref, v_ref, g_ref, cos_ref, sin_ref, intra_ref, qd_ref, kd_ref, cd_ref, gain_ref,
                      o_ref, state_ref):
    @pl.when(pl.program_id(2) == 0)
    def _():
        state_ref[...] = jnp.zeros_like(state_ref)

    intra, q_decay, k_decay, chunk_decay = intra_ref[0], qd_ref[0], kd_ref[0], cd_ref[0]
    for c in range(RET_ROWS // RET_CHUNK):
        rows = pl.ds(c * RET_CHUNK, RET_CHUNK)
        cos, sin = cos_ref[rows, :], sin_ref[rows, :]
        qc = _rotary(q_ref[0, rows, :].astype(F32), cos, sin) * (RET_QK ** -0.5)
        kc = _rotary(k_ref[0, rows, :].astype(F32), cos, sin)
        vc = v_ref[0, rows, :]
        state = state_ref[...]
        scores = lax.dot_general(qc.astype(BF16), kc.astype(BF16), _NT, preferred_element_type=F32) * intra
        y = _dot(scores.astype(BF16), vc) + _dot((qc * q_decay).astype(BF16), state.astype(BF16))
        state_ref[...] = state * chunk_decay + lax.dot_general(
            (kc * k_decay).astype(BF16), vc, _TN, preferred_element_type=F32)
        mu = jnp.mean(y, axis=-1, keepdims=True)
        yc = y - mu
        var = jnp.mean(yc * yc, axis=-1, keepdims=True)
        yn = yc * lax.rsqrt(var + GN_EPS) * gain_ref[...]
        g = g_ref[0, rows, :].astype(F32)
        o_ref[0, rows, :] = (g / (1.0 + jnp.exp(-g)) * yn).astype(o_ref.dtype)


def _retention(proj, gn_gain, batch, seq):
    cos, sin, intra, q_decay, k_decay, chunk_decay = _retention_tables(seq)
    qk_blocks = D_MODEL // RET_QK
    v_blocks = 2 * D_MODEL // RET_V
    per_head = lambda shape: pl.BlockSpec((1,) + shape, lambda b, h, r: (h, 0, 0))
    return pl.pallas_call(
        _retention_kernel,
        grid=(batch, RET_HEADS, seq // RET_ROWS),
        in_specs=[
            pl.BlockSpec((1, RET_ROWS, RET_QK), lambda b, h, r: (b, r, h)),
            pl.BlockSpec((1, RET_ROWS, RET_QK), lambda b, h, r: (b, r, qk_blocks + h)),
            pl.BlockSpec((1, RET_ROWS, RET_V), lambda b, h, r: (b, r, v_blocks + h)),
            pl.BlockSpec((1, RET_ROWS, RET_V), lambda b, h, r: (b, r, v_blocks + RET_HEADS + h)),
            pl.BlockSpec((RET_ROWS, RET_QK // 2), lambda b, h, r: (r, 0)),
            pl.BlockSpec((RET_ROWS, RET_QK // 2), lambda b, h, r: (r, 0)),
            per_head((RET_CHUNK, RET_CHUNK)),
            per_head((RET_CHUNK, 1)),
            per_head((RET_CHUNK, 1)),
            per_head((1, RET_V)),
            pl.BlockSpec((1, RET_V), lambda b, h, r: (0, h)),
        ],
        out_specs=pl.BlockSpec((1, RET_ROWS, RET_V), lambda b, h, r: (b, r, h)),
        out_shape=jax.ShapeDtypeStruct((batch, seq, RET_HEADS * RET_V), BF16),
        scratch_shapes=[pltpu.VMEM((RET_QK, RET_V), F32)],
        compiler_params=_params("parallel", "parallel", "arbitrary"),
    )(proj, proj, proj, proj, cos, sin, intra, q_decay, k_decay, chunk_decay,
      gn_gain.astype(F32).reshape(1, RET_HEADS * RET_V))


def _sb_mixer(x, norm, w_in, batch, seq):
    qkv = _norm_matmul(x, norm, w_in.astype(BF16), BF16, 1024)
    return _sb_attention(qkv.reshape(batch, seq, -1), batch, seq).reshape(batch * seq, -1)


def _retention_mixer(x, norm, w_in, gn_gain, batch, seq):
    proj = _norm_matmul(x, norm, w_in.astype(BF16), BF16, 1024)
    return _retention(proj.reshape(batch, seq, -1), gn_gain, batch, seq).reshape(batch * seq, -1)


def _fox_mixer(x, norm, w_in, b_forget, batch, seq):
    w_qkv = w_in[:, :3 * D_MODEL].astype(BF16)
    w_f = jnp.pad(w_in[:, 3 * D_MODEL:], ((0, 0), (0, LANES - ATT_HEADS))).astype(BF16)
    qkv, f_logits = _norm_matmul_gate(x, norm, w_qkv, w_f, 1024)
    aug_q, aug_k, cum_last = _fox_cum(f_logits.reshape(batch, seq, LANES), b_forget, batch, seq)
    last_table = cum_last[:, :, 0, :ATT_HEADS].transpose(0, 2, 1).reshape(-1)
    o = _fox_attention(qkv.reshape(batch, seq, -1), aug_q, aug_k, last_table, batch, seq)
    return o.reshape(batch * seq, -1)


def kernel(x, norm_mix_0, w_in_0, w_out_0, norm_mlp_0, w_up_0, w_down_0, norm_mix_1, w_in_1, gn_gain_1, w_out_1, norm_mlp_1, w_up_1, w_down_1, norm_mix_2, w_in_2, b_forget_2, w_out_2, norm_mlp_2, w_up_2, w_down_2, norm_mix_3, w_in_3, w_out_3, norm_mlp_3, w_up_3, w_down_3, norm_final):
    batch, seq, d = x.shape
    h = x.reshape(batch * seq, d)
    tails = ((w_out_0, norm_mlp_0, w_up_0, w_down_0), (w_out_1, norm_mlp_1, w_up_1, w_down_1),
             (w_out_2, norm_mlp_2, w_up_2, w_down_2), (w_out_3, norm_mlp_3, w_up_3, w_down_3))
    for layer in range(4):
        if layer == 0:
            a = _sb_mixer(h, norm_mix_0, w_in_0, batch, seq)
        elif layer == 1:
            a = _retention_mixer(h, norm_mix_1, w_in_1, gn_gain_1, batch, seq)
        elif layer == 2:
            a = _fox_mixer(h, norm_mix_2, w_in_2, b_forget_2, batch, seq)
        else:
            a = _sb_mixer(h, norm_mix_3, w_in_3, batch, seq)
        w_out, norm_mlp, w_up, w_down = tails[layer]
        h = _mixer_out_mlp(a, w_out.astype(BF16), h, norm_mlp, w_up.astype(BF16), w_down.astype(BF16),
                           norm_final, layer == 3)
    return h.reshape(batch, seq, d)
```
